```python
import math
import jax
import jax.numpy as jnp
from jax import lax
import numpy as np

D_MODEL = 1024
BATCH = 4
SEQ = 8192
DEPTH = 4
DEC_BATCH = 32
DEC_SEQ = 32
PAST_LEN = 2048

CHUNK = 64
QBLOCK = 128
H_A = 4
DH_A = 64
DV_A = 2 * DH_A
W_A = H_A * DV_A
N_QK = H_A * 2 * DH_A
N_ATT = 2 * N_QK + W_A
H_B = 8
DH_B = 64
W_B = H_B * DH_B
W_LORA = 64
A_LORA = 64
G_LORA = 128
N_RWKV = 3 * W_B + W_LORA + A_LORA + G_LORA
N_IN = N_ATT + N_RWKV
D_FF = 2816
N_BUCKETS = 32
MAX_DISTANCE = 128
ALPHA = (2 * DEPTH) ** 0.25
BETA = (8 * DEPTH) ** -0.25
LN_EPS = 1e-5
GN_EPS = 64e-5

kernel_name = "streaming_diffattn_rwkv7_hybrid_step"


def layer_norm(x, g=None, b=None, eps=LN_EPS):
    xf = x.astype(jnp.float32)
    mu = jnp.mean(xf, axis=-1, keepdims=True)
    var = jnp.mean(jnp.square(xf - mu), axis=-1, keepdims=True)
    y = ((xf - mu) * lax.rsqrt(var + eps)).astype(x.dtype)
    if g is None:
        return y
    return y * g + b


def rms_norm(x, g, eps=LN_EPS):
    xf = x.astype(jnp.float32)
    y = (xf * lax.rsqrt(jnp.mean(jnp.square(xf), axis=-1, keepdims=True) + eps)).astype(x.dtype)
    return y * g


def rel_bucket(rel):
    nb = N_BUCKETS // 2
    ret = jnp.where(rel > 0, nb, 0)
    n = jnp.abs(rel)
    max_exact = nb // 2
    nf = jnp.maximum(n, 1).astype(jnp.float32)
    large = max_exact + (jnp.log(nf / max_exact) / math.log(MAX_DISTANCE / max_exact)
                         * (nb - max_exact)).astype(jnp.int32)
    large = jnp.minimum(large, nb - 1)
    return ret + jnp.where(n < max_exact, n, large)


def diff_attn_block(q, k, v, q_pos, k_pos, rel_bias, lam):
    s = jnp.einsum("bqhmd,bkhmd->bhmqk", q, k).astype(jnp.float32) * (DH_A ** -0.5)
    bias = jnp.take(rel_bias, rel_bucket(k_pos[None, :] - q_pos[:, None]), axis=0)
    s = s + jnp.transpose(bias, (2, 0, 1)).astype(jnp.float32)[None, :, None]
    visible = (k_pos[None, :] // CHUNK) <= (q_pos[:, None] // CHUNK)
    s = jnp.where(visible, s, -jnp.inf)
    p = jax.nn.softmax(s, axis=-1)
    a = (p[:, :, 0] - lam * p[:, :, 1]).astype(v.dtype)
    return jnp.einsum("bhqk,bkhd->bqhd", a, v)


def diff_attention(q, k, v, q_pos, k_pos, rel_bias, lam):
    b, sq = q.shape[:2]
    qb = QBLOCK if sq % QBLOCK == 0 else sq
    nblk = sq // qb
    q_blocks = jnp.swapaxes(q.reshape(b, nblk, qb, H_A, 2, DH_A), 0, 1)
    pos_blocks = q_pos.reshape(nblk, qb)
    out = lax.map(lambda qp: diff_attn_block(qp[0], k, v, qp[1], k_pos, rel_bias, lam),
                  (q_blocks, pos_blocks))
    return jnp.swapaxes(out, 0, 1).reshape(b, sq, H_A, DV_A)


def diff_mixer(p_att, past_k, past_v, q_pos, k_pos, lam_qk, subln_g, rel_bias, layer_idx):
    b, s = p_att.shape[:2]
    q = p_att[..., :N_QK].reshape(b, s, H_A, 2, DH_A)
    k_new = p_att[..., N_QK:2 * N_QK].reshape(b, s, H_A, 2 * DH_A)
    v_new = p_att[..., 2 * N_QK:].reshape(b, s, H_A, DV_A)
    if past_k is None:
        k_all, v_all = k_new, v_new
    else:
        k_all = jnp.concatenate([past_k.astype(k_new.dtype), k_new], axis=1)
        v_all = jnp.concatenate([past_v.astype(v_new.dtype), v_new], axis=1)
    lam_init = 0.8 - 0.6 * math.exp(-0.3 * layer_idx)
    lq = lam_qk.astype(jnp.float32)
    lam = jnp.exp(jnp.sum(lq[0] * lq[1])) - jnp.exp(jnp.sum(lq[2] * lq[3])) + lam_init
    o = diff_attention(q, k_all.reshape(b, -1, H_A, 2, DH_A), v_all, q_pos, k_pos, rel_bias, lam)
    o = rms_norm(o, subln_g) * (1.0 - lam_init)
    return o.reshape(b, s, W_A), k_new, v_new


def wkv_step(state, inp):
    r_t, w_t, k_t, v_t, kk_t, a_t = inp
    sa = jnp.einsum("bhvk,bhk->bhv", state, -kk_t)
    state = (state * w_t[:, :, None, :] + sa[..., None] * (kk_t * a_t)[:, :, None, :]
             + v_t[..., None] * k_t[:, :, None, :])
    return state, jnp.einsum("bhvk,bhk->bhv", state, r_t)


def rwkv_mixer(p_rw, shift, wkv, mu, w0, w2, a0, a2, g2, k_k, k_a, r_k, lnx_g, lnx_b):
    b, s = p_rw.shape[:2]
    prev = jnp.concatenate([shift[:, None].astype(p_rw.dtype), p_rw[:, :-1]], axis=1)
    xm = p_rw + (prev - p_rw) * mu
    o0 = 3 * W_B
    r = xm[..., :W_B]
    k = xm[..., W_B:2 * W_B]
    v = xm[..., 2 * W_B:o0]
    dw = xm[..., o0:o0 + W_LORA]
    da = xm[..., o0 + W_LORA:o0 + W_LORA + A_LORA]
    dg = xm[..., o0 + W_LORA + A_LORA:]
    w_log = -jax.nn.softplus(-(w0 + jnp.tanh(dw) @ w2)) - 0.5
    decay = jnp.exp(-jnp.exp(w_log.astype(jnp.float32)))
    a = jax.nn.sigmoid(a0 + da @ a2)
    g = jax.nn.sigmoid(dg) @ g2

    def heads(t):
        return t.reshape(b, s, H_B, DH_B)

    kk = heads(k * k_k).astype(jnp.float32)
    kk = kk * lax.rsqrt(jnp.maximum(jnp.sum(kk * kk, axis=-1, keepdims=True), 1e-24))
    k = k * (1.0 + (a - 1.0) * k_a)
    rh, kh, vh, ah = heads(r), heads(k), heads(v), heads(a)

    def tmajor(t):
        return jnp.swapaxes(t.astype(jnp.float32), 0, 1)

    xs = (tmajor(rh), tmajor(heads(decay)), tmajor(kh), tmajor(vh), tmajor(kk), tmajor(ah))
    s_final, o = lax.scan(wkv_step, wkv.astype(jnp.float32), xs)
    o = jnp.swapaxes(o, 0, 1)
    m = jnp.mean(o, axis=-1, keepdims=True)
    var = jnp.mean(jnp.square(o - m), axis=-1, keepdims=True)
    o = ((o - m) * lax.rsqrt(var + GN_EPS)).astype(p_rw.dtype).reshape(b, s, W_B) * lnx_g + lnx_b
    bonus = jnp.sum(rh * kh * r_k, axis=-1, keepdims=True) * vh
    o = (o + bonus.reshape(b, s, W_B)) * g
    return o, p_rw[:, -1], s_final.astype(p_rw.dtype)


def swiglu(u, w_gu, w_down):
    h = u @ w_gu
    return (jax.nn.silu(h[..., :D_FF]) * h[..., D_FF:]) @ w_down


def trunk_layer(x, c, l, W, rel_bias, past_k, past_v, wkv, shift, q_pos, k_pos):
    b = x.shape[0]
    mod = (jax.nn.silu(c) @ W["w_ada"][l] + W["b_ada"][l]).reshape(b, 3, 3, D_MODEL)

    def adaln(h, i):
        return layer_norm(h) * (1.0 + mod[:, i, 1, None]) + mod[:, i, 0, None]

    f = swiglu(adaln(x, 0), W["w_gu"][l, 0], W["w_down"][l, 0])
    x = layer_norm(ALPHA * x + 0.5 * mod[:, 0, 2, None] * f, W["ln_g"][l, 0], W["ln_b"][l, 0])
    u = adaln(x, 1)
    proj = u @ W["w_in"][l]
    y_a, k_new, v_new = diff_mixer(proj[..., :N_ATT], past_k, past_v, q_pos, k_pos,
                                   W["lam_qk"][l], W["subln_g"][l], rel_bias, l)
    y_b, shift_new, wkv_new = rwkv_mixer(proj[..., N_ATT:], shift, wkv, W["rw_mu"][l], W["rw_w0"][l],
                                         W["rw_w2"][l], W["rw_a0"][l], W["rw_a2"][l], W["rw_g2"][l],
                                         W["rw_k_k"][l], W["rw_k_a"][l], W["rw_r_k"][l],
                                         W["rw_lnx_g"][l], W["rw_lnx_b"][l])
    gates = jax.nn.sigmoid(u @ W["w_gate"][l] + W["b_gate"][l])
    merged = (gates[..., :D_MODEL] * (y_a @ W["w_br_a"][l])
              + gates[..., D_MODEL:] * (y_b @ W["w_br_b"][l]))
    x = layer_norm(ALPHA * x + mod[:, 1, 2, None] * (merged @ W["w_o"][l]),
                   W["ln_g"][l, 1], W["ln_b"][l, 1])
    f = swiglu(adaln(x, 2), W["w_gu"][l, 1], W["w_down"][l, 1])
    x = layer_norm(ALPHA * x + 0.5 * mod[:, 2, 2, None] * f, W["ln_g"][l, 2], W["ln_b"][l, 2])
    return x, k_new, v_new, wkv_new, shift_new


def setup_inputs(seed: int = 0) -> dict:
    key = jax.random.key(seed)
    keys = iter(jax.random.split(key, 40))

    def nrm(shape, scale):
        return scale * jax.random.normal(next(keys), shape, jnp.float32)

    def uni(shape, lo, hi):
        return jax.random.uniform(next(keys), shape, jnp.float32, lo, hi)

    L, D = DEPTH, D_MODEL
    return {
        "x_prompt": nrm((BATCH, SEQ, D), 1.0),
        "x_sample": nrm((DEC_BATCH, DEC_SEQ, D), 1.0),
        "c_prompt": nrm((BATCH, D), 1.0),
        "c_sample": nrm((DEC_BATCH, D), 1.0),
        "cache_k": nrm((L, DEC_BATCH, PAST_LEN, H_A, 2 * DH_A), 1.0),
        "cache_v": nrm((L, DEC_BATCH, PAST_LEN, H_A, DV_A), 1.0),
        "state_wkv": nrm((L, DEC_BATCH, H_B, DH_B, DH_B), 0.5),
        "state_shift": nrm((L, DEC_BATCH, N_RWKV), 1.0),
        "rel_bias": nrm((N_BUCKETS, H_A), 0.5),
        "w_ada": nrm((L, D, 9 * D), 0.5 * D ** -0.5),
        "b_ada": nrm((L, 9 * D), 0.01),
        "ln_g": 1.0 + nrm((L, 3, D), 0.05),
        "ln_b": nrm((L, 3, D), 0.01),
        "w_gu": nrm((L, 2, D, 2 * D_FF), D ** -0.5),
        "w_down": nrm((L, 2, D_FF, D), BETA * D_FF ** -0.5),
        "w_in": nrm((L, D, N_IN), D ** -0.5),
        "lam_qk": nrm((L, 4, DH_A), 0.1),
        "subln_g": 1.0 + nrm((L, DV_A), 0.1),
        "rw_mu": uni((L, N_RWKV), 0.0, 1.0),
        "rw_w0": uni((L, W_B), -4.0, 1.0),
        "rw_w2": nrm((L, W_LORA, W_B), 0.1 * W_LORA ** -0.5),
        "rw_a0": nrm((L, W_B), 0.1),
        "rw_a2": nrm((L, A_LORA, W_B), 0.1 * A_LORA ** -0.5),
        "rw_g2": nrm((L, G_LORA, W_B), G_LORA ** -0.5),
        "rw_k_k": 0.85 + nrm((L, W_B), 0.05),
        "rw_k_a": 1.0 + nrm((L, W_B), 0.05),
        "rw_r_k": nrm((L, H_B, DH_B), 0.1),
        "rw_lnx_g": 1.0 + nrm((L, W_B), 0.1),
        "rw_lnx_b": nrm((L, W_B), 0.01),
        "w_br_a": nrm((L, W_A, D), BETA * W_A ** -0.5),
        "w_br_b": nrm((L, W_B, D), BETA * W_B ** -0.5),
        "w_gate": nrm((L, D, 2 * D), D ** -0.5),
        "b_gate": nrm((L, 2 * D), 0.01),
        "w_o": nrm((L, D, D), BETA * D ** -0.5),
    }


def reference(x_prompt, x_sample, c_prompt, c_sample, cache_k, cache_v, state_wkv, state_shift,
              rel_bias, w_ada, b_ada, ln_g, ln_b, w_gu, w_down, w_in, lam_qk, subln_g,
              rw_mu, rw_w0, rw_w2, rw_a0, rw_a2, rw_g2, rw_k_k, rw_k_a, rw_r_k, rw_lnx_g, rw_lnx_b,
              w_br_a, w_br_b, w_gate, b_gate, w_o):
    W = {"w_ada": w_ada, "b_ada": b_ada, "ln_g": ln_g, "ln_b": ln_b, "w_gu": w_gu,
         "w_down": w_down, "w_in": w_in, "lam_qk": lam_qk, "subln_g": subln_g,
         "rw_mu": rw_mu, "rw_w0": rw_w0, "rw_w2": rw_w2, "rw_a0": rw_a0, "rw_a2": rw_a2,
         "rw_g2": rw_g2, "rw_k_k": rw_k_k, "rw_k_a": rw_k_a, "rw_r_k": rw_r_k,
         "rw_lnx_g": rw_lnx_g, "rw_lnx_b": rw_lnx_b, "w_br_a": w_br_a, "w_br_b": w_br_b,
         "w_gate": w_gate, "b_gate": b_gate, "w_o": w_o}
    b_p, s_p = x_prompt.shape[:2]
    past = cache_k.shape[2]
    s_d = x_sample.shape[1]
    pos_p = jnp.arange(s_p, dtype=jnp.int32)
    pos_d = past + jnp.arange(s_d, dtype=jnp.int32)
    k_pos_d = jnp.concatenate([jnp.arange(past, dtype=jnp.int32), pos_d])
    zero_wkv = jnp.zeros((b_p, H_B, DH_B, DH_B), x_prompt.dtype)
    zero_shift = jnp.zeros((b_p, N_RWKV), x_prompt.dtype)

    hp, hd = x_prompt, x_sample
    kps, vps, sps, shps = [], [], [], []
    kds, vds, sds, shds = [], [], [], []
    for l in range(DEPTH):
        hp, kp, vp, sp, shp = trunk_layer(hp, c_prompt, l, W, rel_bias, None, None,
                                          zero_wkv, zero_shift, pos_p, pos_p)
        hd, kd, vd, sd, shd = trunk_layer(hd, c_sample, l, W, rel_bias, cache_k[l], cache_v[l],
                                          state_wkv[l], state_shift[l], pos_d, k_pos_d)
        kps.append(kp); vps.append(vp); sps.append(sp); shps.append(shp)
        kds.append(kd); vds.append(vd); sds.append(sd); shds.append(shd)
    return (hp, hd, jnp.stack(kps), jnp.stack(vps), jnp.stack(sps), jnp.stack(shps),
            jnp.stack(kds), jnp.stack(vds), jnp.stack(sds), jnp.stack(shds))
```

```python
import functools
import math

import jax
import jax.numpy as jnp
from jax import lax
from jax.experimental import pallas as pl
from jax.experimental.pallas import tpu as pltpu

F32 = jnp.float32
BF16 = jnp.bfloat16
HI = lax.Precision.HIGHEST

D_MODEL = 1024
CHUNK = 64
H_A = 4
DH_A = 64
DV_A = 128
N_QK = 512
N_ATT = 1536
H_B = 8
DH_B = 64
W_B = 512
W_LORA = 64
A_LORA = 64
G_LORA = 128
N_RWKV = 1792
D_FF = 2816
N_BUCKETS = 32
LN_EPS = 1e-5
GN_EPS = 64e-5

LANES = 128
VMEM_LIMIT = 56 * 1024 * 1024
NEG_BIG = -1e30
FAR_BUCKET = N_BUCKETS // 2 - 1
BUCKET_STARTS = (12, 16, 23, 32, 46, 64, 91)

SOLVE_BLOCK = 16


def _params(*sem):
    return pltpu.CompilerParams(dimension_semantics=sem, vmem_limit_bytes=VMEM_LIMIT)


def _norm(x):
    mu = jnp.mean(x, axis=-1, keepdims=True)
    xc = x - mu
    var = jnp.mean(xc * xc, axis=-1, keepdims=True)
    return xc * lax.rsqrt(var + LN_EPS)


def _sigmoid(x):
    return 1.0 / (1.0 + jnp.exp(-x))


def _div_pow2(x, n):
    assert n & (n - 1) == 0
    return lax.shift_right_logical(x, n.bit_length() - 1)


def _mod_rows(mod_ref, sub):
    return tuple(mod_ref[:, 3 * sub + j:3 * sub + j + 1, :] for j in range(3))


def _ada_kernel(c_ref, w_ref, b_ref, o_ref):
    c = c_ref[...]
    sc = (c * _sigmoid(c)).astype(BF16)
    o_ref[0] = jnp.dot(sc, w_ref[0].astype(BF16), preferred_element_type=F32) + b_ref[0]


def _ada_call(c, w_ada, b_ada):
    depth, d, n = w_ada.shape
    nseq = c.shape[0]
    tn = 1152
    return pl.pallas_call(
        _ada_kernel,
        grid=(depth, n // tn),
        in_specs=[pl.BlockSpec((nseq, d), lambda l, j: (0, 0)),
                  pl.BlockSpec((1, d, tn), lambda l, j: (l, 0, j)),
                  pl.BlockSpec((1, 1, tn), lambda l, j: (l, 0, j))],
        out_specs=pl.BlockSpec((1, nseq, tn), lambda l, j: (l, 0, j)),
        out_shape=jax.ShapeDtypeStruct((depth, nseq, n), F32),
        compiler_params=_params("parallel", "parallel"),
        name="ada_mod",
    )(c, w_ada, b_ada.reshape(depth, 1, n))


def _ffn_kernel(x_ref, mod_ref, wgu_ref, wd_ref, g_ref, b_ref, o_ref, *, sub, fc, alpha):
    g_, ts, d = x_ref.shape
    x = x_ref[...]
    shift, scale, gate = _mod_rows(mod_ref, sub)
    u = (_norm(x) * (1.0 + scale) + shift).reshape(g_ * ts, d).astype(BF16)
    f = jnp.zeros((g_ * ts, d), F32)
    for c in range(D_FF // fc):
        hg = jnp.dot(u, wgu_ref[:, c * fc:(c + 1) * fc], preferred_element_type=F32)
        hu = jnp.dot(u, wgu_ref[:, D_FF + c * fc:D_FF + (c + 1) * fc], preferred_element_type=F32)
        act = (hg * _sigmoid(hg) * hu).astype(BF16)
        f = f + jnp.dot(act, wd_ref[c * fc:(c + 1) * fc, :], preferred_element_type=F32)
    y = alpha * x + (0.5 * gate) * f.reshape(g_, ts, d)
    o_ref[...] = _norm(y) * g_ref[...] + b_ref[...]


def _seq_tiling(nseq, s):
    if s >= 512:
        return 1, 512
    g = max(1, min(nseq, 256 // s))
    while nseq % g:
        g -= 1
    return g, s


def _ffn_call(x, mod, wgu, wd, ln_g, ln_b, sub, alpha):
    nseq, s, d = x.shape
    g_, ts = _seq_tiling(nseq, s)
    const = lambda i, j: (0, 0)
    return pl.pallas_call(
        functools.partial(_ffn_kernel, sub=sub, fc=D_FF // 2, alpha=alpha),
        grid=(nseq // g_, s // ts),
        in_specs=[pl.BlockSpec((g_, ts, d), lambda i, j: (i, j, 0)),
                  pl.BlockSpec((g_, 9, d), lambda i, j: (i, 0, 0)),
                  pl.BlockSpec(wgu.shape, const, pipeline_mode=pl.Buffered(1)),
                  pl.BlockSpec(wd.shape, const, pipeline_mode=pl.Buffered(1)),
                  pl.BlockSpec((1, d), const),
                  pl.BlockSpec((1, d), const)],
        out_specs=pl.BlockSpec((g_, ts, d), lambda i, j: (i, j, 0)),
        out_shape=jax.ShapeDtypeStruct(x.shape, F32),
        compiler_params=_params("parallel", "parallel"),
        name="ffn_block",
    )(x, mod, wgu, wd, ln_g.reshape(1, d), ln_b.reshape(1, d))


def _mixin_kernel(x_ref, mod_ref, win_ref, q_ref, k_ref, v_ref, kb_ref, vb_ref, prw_ref):
    g_, ts, d = x_ref.shape
    shift, scale, _ = _mod_rows(mod_ref, 1)
    u = (_norm(x_ref[...]) * (1.0 + scale) + shift).reshape(g_ * ts, d).astype(BF16)

    def proj(lo, hi):
        return jnp.dot(u, win_ref[:, lo:hi], preferred_element_type=F32).reshape(g_, ts, hi - lo)

    q = proj(0, N_QK)
    q_ref[...] = (q * (DH_A ** -0.5)).astype(BF16)
    k = proj(N_QK, 2 * N_QK)
    k_ref[...] = k
    kb_ref[...] = k.astype(BF16)
    v = proj(2 * N_QK, N_ATT)
    v_ref[...] = v
    vb_ref[...] = v.astype(BF16)
    prw_ref[...] = proj(N_ATT, N_ATT + N_RWKV)


def _mixin_call(x, mod, win):
    nseq, s, d = x.shape
    g_, ts = _seq_tiling(nseq, s)
    const = lambda i, j: (0, 0)
    blk = lambda w: pl.BlockSpec((g_, ts, w), lambda i, j: (i, j, 0))
    sds = lambda w, dt: jax.ShapeDtypeStruct((nseq, s, w), dt)
    return pl.pallas_call(
        _mixin_kernel,
        grid=(nseq // g_, s // ts),
        in_specs=[blk(d),
                  pl.BlockSpec((g_, 9, d), lambda i, j: (i, 0, 0)),
                  pl.BlockSpec(win.shape, const, pipeline_mode=pl.Buffered(1))],
        out_specs=[blk(N_QK), blk(N_QK), blk(N_QK), blk(N_QK), blk(N_QK), blk(N_RWKV)],
        out_shape=[sds(N_QK, BF16), sds(N_QK, F32), sds(N_QK, F32), sds(N_QK, BF16), sds(N_QK, BF16),
                   sds(N_RWKV, F32)],
        compiler_params=_params("parallel", "parallel"),
        name="mixer_in",
    )(x, mod, win)


def _bias_tile(rb_ref, h, tq, tk, offset, valid_cols):
    r = lax.broadcasted_iota(jnp.int32, (tq, tk), 0)
    c = lax.broadcasted_iota(jnp.int32, (tq, tk), 1)
    rel = c - r + offset
    n = jnp.abs(rel)
    log_bucket = jnp.full((tq, tk), N_BUCKETS // 4, jnp.int32)
    for start in BUCKET_STARTS:
        log_bucket = log_bucket + jnp.where(n >= start, 1, 0)
    bucket = jnp.where(rel > 0, N_BUCKETS // 2, 0) + jnp.where(n < N_BUCKETS // 4, n, log_bucket)
    bias = jnp.zeros((tq, tk), F32)
    for b in range(N_BUCKETS):
        bias = jnp.where(bucket == b, rb_ref[b * H_A + h], bias)
    if offset == 0:
        visible = jnp.logical_and(_div_pow2(c, CHUNK) <= _div_pow2(r, CHUNK), c < valid_cols)
        bias = jnp.where(visible, bias, NEG_BIG)
    return bias


def _attn_kernel(rb_ref, q_ref, k_ref, v_ref, lam_ref, g_ref, o_ref, m_scr, l_scr, acc_scr, bias_scr,
                 *, tq, tk, first_tile, valid_cols, lam_init):
    h = pl.program_id(1)
    i = pl.program_id(2)

    @pl.when(i == 0)
    def _():
        for slot, offset in ((0, -tk), (1, 0)):
            tile = _bias_tile(rb_ref, h, tq, tk, offset, valid_cols)
            bias_scr[slot] = jnp.concatenate([tile, tile], axis=0)

    q = q_ref[0]
    lane = lax.broadcasted_iota(jnp.int32, (tq, LANES), 1)
    zero = jnp.zeros_like(q)
    q2 = jnp.concatenate([jnp.where(lane < DH_A, q, zero), jnp.where(lane < DH_A, zero, q)], axis=0)

    m_scr[...] = jnp.full(m_scr.shape, NEG_BIG, F32)
    l_scr[...] = jnp.zeros(l_scr.shape, F32)
    acc_scr[...] = jnp.zeros(acc_scr.shape, F32)

    def tile(j, bias):
        start = pl.multiple_of(j * tk, tk)
        kt = k_ref[0, pl.ds(start, tk), :]
        vt = v_ref[0, pl.ds(start, tk), :]
        s = lax.dot_general(q2, kt, (((1,), (1,)), ((), ())), preferred_element_type=F32) + bias
        m_prev = m_scr[...]
        m_new = jnp.maximum(m_prev, jnp.max(s, axis=1, keepdims=True))
        p = jnp.exp(s - m_new)
        alpha = jnp.exp(m_prev - m_new)
        l_scr[...] = alpha * l_scr[...] + jnp.sum(p, axis=1, keepdims=True)
        acc_scr[...] = alpha * acc_scr[...] + jnp.dot(p.astype(BF16), vt, preferred_element_type=F32)
        m_scr[...] = m_new

    last = first_tile + i
    far_bias = rb_ref[FAR_BUCKET * H_A + h]

    def far_body(j, carry):
        tile(j, far_bias)
        return carry

    lax.fori_loop(0, last - 1, far_body, 0)

    @pl.when(last >= 1)
    def _():
        tile(last - 1, bias_scr[0])

    tile(last, bias_scr[1])

    lq = lam_ref[...]
    lam = (jnp.exp(jnp.sum(lq[0:1] * lq[1:2], axis=1, keepdims=True))
           - jnp.exp(jnp.sum(lq[2:3] * lq[3:4], axis=1, keepdims=True)) + lam_init)
    o_all = acc_scr[...] / l_scr[...]
    o = o_all[:tq] - lam * o_all[tq:]
    o = o * lax.rsqrt(jnp.mean(o * o, axis=-1, keepdims=True) + LN_EPS)
    o_ref[0] = (o * g_ref[...] * (1.0 - lam_init)).astype(o_ref.dtype)


def _attn_call(q, k, v, rel_bias, lam_qk, subln_g, q_off, sk_valid, lam_init):
    b, sq, _ = q.shape
    sk = k.shape[1]
    tk = LANES
    tq = min(tk, sq)
    nq = sq // tq
    n_tiles = sk // tk
    valid_cols = sk_valid - (n_tiles - 1) * tk
    assert sq % tq == 0 and sk % tk == 0 and q_off % tk == 0 and 0 < valid_cols <= tk
    assert (nq == 1 or tq == tk) and (valid_cols == tk or nq == 1)
    assert tq % CHUNK == 0 or nq == 1
    assert q_off // tk + nq == n_tiles
    kern = functools.partial(_attn_kernel, tq=tq, tk=tk, first_tile=q_off // tk, valid_cols=valid_cols,
                             lam_init=lam_init)
    return pl.pallas_call(
        kern,
        grid=(b, H_A, nq),
        in_specs=[pl.BlockSpec(memory_space=pltpu.SMEM),
                  pl.BlockSpec((1, tq, LANES), lambda b_, h, i: (b_, i, h)),
                  pl.BlockSpec((1, sk, LANES), lambda b_, h, i: (b_, 0, h)),
                  pl.BlockSpec((1, sk, LANES), lambda b_, h, i: (b_, 0, h)),
                  pl.BlockSpec((4, DH_A), lambda b_, h, i: (0, 0)),
                  pl.BlockSpec((1, DV_A), lambda b_, h, i: (0, 0))],
        out_specs=pl.BlockSpec((1, tq, LANES), lambda b_, h, i: (b_, i, h)),
        out_shape=jax.ShapeDtypeStruct((b, sq, H_A * DV_A), BF16),
        scratch_shapes=[pltpu.VMEM((2 * tq, tk), F32), pltpu.VMEM((2 * tq, tk), F32),
                        pltpu.VMEM((2 * tq, DV_A), F32), pltpu.VMEM((2, 2 * tq, tk), F32)],
        compiler_params=_params("parallel", "parallel", "arbitrary"),
        name="diff_attn",
    )(rel_bias.reshape(-1), q, k, v, lam_qk, subln_g.reshape(1, DV_A))


def _dot(a, b):
    return jnp.dot(a, b, precision=HI, preferred_element_type=F32)


def _dot_nt(a, b):
    return lax.dot_general(a, b, (((1,), (1,)), ((), ())), precision=HI, preferred_element_type=F32)


def _dot_tn(a, b):
    return lax.dot_general(a, b, (((0,), (0,)), ((), ())), precision=HI, preferred_element_type=F32)


def _rwkv_kernel(prw_ref, shift_ref, s0_ref, mu_ref, wwa_ref, g2_ref, vec_ref, bd_ref,
                 y_ref, sout_ref, carry_scr, state_scr, *, c_len):
    ci = pl.program_id(1)
    nc = pl.num_programs(1)
    n_pairs = H_B // 2

    @pl.when(ci == 0)
    def _():
        carry_scr[...] = shift_ref[0]
        state_scr[...] = s0_ref[0]

    p = prw_ref[0]
    row = lax.broadcasted_iota(jnp.int32, p.shape, 0)
    prev = jnp.where(row == 0, carry_scr[...], pltpu.roll(p, 1, axis=0))
    carry_scr[...] = p[c_len - 1:c_len]
    xm = p + (prev - p) * mu_ref[...]

    w0, a0, k_k, k_a, r_k, lnx_g, lnx_b = (vec_ref[j:j + 1] for j in range(7))
    r = xm[:, :W_B]
    k = xm[:, W_B:2 * W_B]
    v = xm[:, 2 * W_B:3 * W_B]
    lora_in = xm[:, 3 * W_B:3 * W_B + W_LORA + A_LORA]
    lane = lax.broadcasted_iota(jnp.int32, lora_in.shape, 1)
    lora = _dot(jnp.where(lane < W_LORA, jnp.tanh(lora_in), lora_in), wwa_ref[...])
    ww = -(w0 + lora[:, :W_B])
    w_log = -(jnp.maximum(ww, 0.0) + jnp.log(1.0 + jnp.exp(-jnp.abs(ww)))) - 0.5
    lw = -jnp.exp(w_log)
    a = _sigmoid(a0 + lora[:, W_B:])
    g = _dot(_sigmoid(xm[:, 3 * W_B + W_LORA + A_LORA:]), g2_ref[...])

    bd = bd_ref[...]
    kk = k * k_k
    kk = kk * lax.rsqrt(jnp.maximum(_dot(kk * kk, bd), 1e-24))
    km = k * (1.0 + (a - 1.0) * k_a)
    kb = kk * a
    bonus = _dot(r * km * r_k, bd) * v

    c2 = 2 * c_len
    tri_r = lax.broadcasted_iota(jnp.int32, (c_len, c_len), 0)
    tri_c = lax.broadcasted_iota(jnp.int32, (c_len, c_len), 1)
    tril = jnp.where(tri_r >= tri_c, 1.0, 0.0).astype(F32)
    rr = lax.broadcasted_iota(jnp.int32, (c2, c2), 0)
    cc = lax.broadcasted_iota(jnp.int32, (c2, c2), 1)
    strict = rr > cc
    incl = rr >= cc
    eye = rr == cc
    same_blk = _div_pow2(rr, SOLVE_BLOCK) == _div_pow2(cc, SOLVE_BLOCK)
    head0 = lax.broadcasted_iota(jnp.int32, (c_len, LANES), 1) < DH_B

    def stack(x):
        z = jnp.zeros_like(x)
        return jnp.concatenate([jnp.where(head0, x, z), jnp.where(head0, z, x)], axis=0)

    def plus_eye(x):
        return jnp.where(eye, x + 1.0, x)

    outs = []
    for pr in range(n_pairs):
        sl = slice(pr * LANES, (pr + 1) * LANES)
        lw_p = lw[:, sl]
        cum = _dot(tril, lw_p)
        g_in = jnp.exp(cum)
        g_ex = jnp.exp(cum - lw_p)
        g_inv = jnp.exp(-cum)
        g_end = g_in[c_len - 1:c_len]
        a_hat = stack(-kk[:, sl] * g_ex)
        r_hat = stack(r[:, sl] * g_in)
        b_til = stack(kb[:, sl] * g_inv)
        k_til = stack(km[:, sl] * g_inv)
        v2 = stack(v[:, sl])
        zero = jnp.zeros((c2, c2), F32)
        lhs = jnp.concatenate([a_hat, r_hat], axis=0)
        ab_rb = _dot_nt(lhs, b_til)
        ak_rk = _dot_nt(lhs, k_til)
        a_ab = jnp.where(strict, ab_rb[:c2], zero)
        a_rb = jnp.where(incl, ab_rb[c2:], zero)
        a_ak = jnp.where(strict, ak_rk[:c2], zero)
        a_rk = jnp.where(incl, ak_rk[c2:], zero)

        d1 = jnp.where(same_blk, a_ab, zero)
        e = a_ab - d1
        dinv = plus_eye(d1)
        dpow = d1
        for _ in range(int(math.log2(SOLVE_BLOCK)) - 1):
            dpow = _dot(dpow, dpow)
            dinv = dinv + _dot(dinv, dpow)
        n1 = _dot(dinv, e)
        t_mat = dinv
        npow = n1
        n_blocks = c_len // SOLVE_BLOCK
        for step in range(max(1, int(math.ceil(math.log2(n_blocks))))):
            t_mat = t_mat + _dot(npow, t_mat)
            if step + 1 < int(math.ceil(math.log2(n_blocks))):
                npow = _dot(npow, npow)

        s_pair = state_scr[pr]
        u = _dot(t_mat, _dot_nt(a_hat, s_pair) + _dot(a_ak, v2))
        o2 = _dot_nt(r_hat, s_pair) + _dot(a_rk, v2) + _dot(a_rb, u)
        outs.append(o2[:c_len] + o2[c_len:])
        s_new = s_pair + _dot_tn(jnp.concatenate([v2, u], axis=0), jnp.concatenate([k_til, b_til], axis=0))
        state_scr[pr] = s_new * g_end

    o = jnp.concatenate(outs, axis=1)
    inv_dh = 1.0 / DH_B
    mean = _dot(o, bd) * inv_dh
    oc = o - mean
    var = _dot(oc * oc, bd) * inv_dh
    on = oc * lax.rsqrt(var + GN_EPS) * lnx_g + lnx_b
    y_ref[0] = ((on + bonus) * g).astype(y_ref.dtype)

    @pl.when(ci == nc - 1)
    def _():
        sout_ref[0] = state_scr[...]


def _pair_states(wkv):
    b = wkv.shape[0]
    w = wkv.reshape(b, H_B // 2, 2, DH_B, DH_B)
    z = jnp.zeros_like(w[:, :, 0])
    top = jnp.concatenate([w[:, :, 0], z], axis=-1)
    bot = jnp.concatenate([z, w[:, :, 1]], axis=-1)
    return jnp.concatenate([top, bot], axis=-2)


def _unpair_states(sp):
    b = sp.shape[0]
    return jnp.stack([sp[:, :, :DH_B, :DH_B], sp[:, :, DH_B:, DH_B:]], axis=2).reshape(b, H_B, DH_B, DH_B)


def _rwkv_call(prw, shift, wkv, mu, wwa, g2, vecs, bd):
    nseq, s, _ = prw.shape
    c_len = min(CHUNK, s)
    assert s % c_len == 0 and c_len % SOLVE_BLOCK == 0
    const2 = lambda b_, c: (0, 0)
    y, s_out = pl.pallas_call(
        functools.partial(_rwkv_kernel, c_len=c_len),
        grid=(nseq, s // c_len),
        in_specs=[pl.BlockSpec((1, c_len, N_RWKV), lambda b_, c: (b_, c, 0)),
                  pl.BlockSpec((1, 1, N_RWKV), lambda b_, c: (b_, 0, 0)),
                  pl.BlockSpec((1, H_B // 2, LANES, LANES), lambda b_, c: (b_, 0, 0, 0)),
                  pl.BlockSpec((1, N_RWKV), const2),
                  pl.BlockSpec(wwa.shape, const2),
                  pl.BlockSpec(g2.shape, const2),
                  pl.BlockSpec(vecs.shape, const2),
                  pl.BlockSpec(bd.shape, const2)],
        out_specs=[pl.BlockSpec((1, c_len, W_B), lambda b_, c: (b_, c, 0)),
                   pl.BlockSpec((1, H_B // 2, LANES, LANES), lambda b_, c: (b_, 0, 0, 0))],
        out_shape=[jax.ShapeDtypeStruct((nseq, s, W_B), BF16),
                   jax.ShapeDtypeStruct((nseq, H_B // 2, LANES, LANES), F32)],
        scratch_shapes=[pltpu.VMEM((1, N_RWKV), F32), pltpu.VMEM((H_B // 2, LANES, LANES), F32)],
        compiler_params=_params("parallel", "arbitrary"),
        name="rwkv7_chunk",
    )(prw, shift.reshape(nseq, 1, N_RWKV), _pair_states(wkv), mu.reshape(1, N_RWKV), wwa, g2, vecs, bd)
    return y, _unpair_states(s_out)


def _merge_kernel(x_ref, mod_ref, ya_ref, yb_ref, wg_ref, bg_ref, wa_ref, wb_ref, wo_ref, g_ref, b_ref,
                  o_ref, *, alpha):
    g_, ts, d = x_ref.shape
    m = g_ * ts
    x = x_ref[...]
    shift, scale, gate = _mod_rows(mod_ref, 1)
    u = (_norm(x) * (1.0 + scale) + shift).reshape(m, d).astype(BF16)
    ma = jnp.dot(ya_ref[...].reshape(m, -1), wa_ref[...], preferred_element_type=F32)
    mb = jnp.dot(yb_ref[...].reshape(m, -1), wb_ref[...], preferred_element_type=F32)
    ga = _sigmoid(jnp.dot(u, wg_ref[:, :d], preferred_element_type=F32) + bg_ref[:, :d])
    gb = _sigmoid(jnp.dot(u, wg_ref[:, d:], preferred_element_type=F32) + bg_ref[:, d:])
    merged = (ga * ma + gb * mb).astype(BF16)
    z = jnp.dot(merged, wo_ref[...], preferred_element_type=F32).reshape(g_, ts, d)
    o_ref[...] = _norm(alpha * x + gate * z) * g_ref[...] + b_ref[...]


def _merge_call(x, mod, ya, yb, wg, bg, wa, wb, wo, ln_g, ln_b, alpha):
    nseq, s, d = x.shape
    g_, ts = _seq_tiling(nseq, s)
    const = lambda i, j: (0, 0)
    blk = lambda w: pl.BlockSpec((g_, ts, w), lambda i, j: (i, j, 0))
    full = lambda a: pl.BlockSpec(a.shape, const, pipeline_mode=pl.Buffered(1))
    return pl.pallas_call(
        functools.partial(_merge_kernel, alpha=alpha),
        grid=(nseq // g_, s // ts),
        in_specs=[blk(d), pl.BlockSpec((g_, 9, d), lambda i, j: (i, 0, 0)), blk(ya.shape[-1]), blk(yb.shape[-1]),
                  full(wg), pl.BlockSpec((1, 2 * d), const), full(wa), full(wb), full(wo),
                  pl.BlockSpec((1, d), const), pl.BlockSpec((1, d), const)],
        out_specs=blk(d),
        out_shape=jax.ShapeDtypeStruct(x.shape, F32),
        compiler_params=_params("parallel", "parallel"),
        name="merge_out",
    )(x, mod, ya, yb, wg, bg.reshape(1, 2 * d), wa, wb, wo, ln_g.reshape(1, d), ln_b.reshape(1, d))


def _layer(x, mod, lw, rel_bias, past_k, past_v, wkv, shift, lam_init, alpha):
    nseq, s, _ = x.shape
    x = _ffn_call(x, mod, lw["w_gu0"], lw["w_down0"], lw["ln_g"][0], lw["ln_b"][0], 0, alpha)
    q, k_new, v_new, kb, vb, prw = _mixin_call(x, mod, lw["w_in"])
    if past_k is None:
        k_all, v_all, q_off, sk_valid = kb, vb, 0, s
    else:
        past = past_k.shape[1]
        sk_valid = past + s
        pad = (-sk_valid) % LANES
        zeros = jnp.zeros((nseq, pad, N_QK), BF16)
        k_all = jnp.concatenate([past_k.reshape(nseq, past, N_QK).astype(BF16), kb, zeros], axis=1)
        v_all = jnp.concatenate([past_v.reshape(nseq, past, N_QK).astype(BF16), vb, zeros], axis=1)
        q_off = past
    ya = _attn_call(q, k_all, v_all, rel_bias, lw["lam_qk"], lw["subln_g"], q_off, sk_valid, lam_init)
    yb, wkv_new = _rwkv_call(prw, shift, wkv, lw["rw_mu"], lw["rw_wwa"], lw["rw_g2"], lw["rw_vecs"], lw["bd"])
    x = _merge_call(x, mod, ya, yb, lw["w_gate"], lw["b_gate"], lw["w_br_a"], lw["w_br_b"], lw["w_o"],
                    lw["ln_g"][1], lw["ln_b"][1], alpha)
    x = _ffn_call(x, mod, lw["w_gu1"], lw["w_down1"], lw["ln_g"][2], lw["ln_b"][2], 2, alpha)
    return (x, k_new.reshape(nseq, s, H_A, 2 * DH_A), v_new.reshape(nseq, s, H_A, DV_A), wkv_new,
            prw[:, -1])


def kernel(x_prompt, x_sample, c_prompt, c_sample, cache_k, cache_v, state_wkv, state_shift, rel_bias, w_ada, b_ada, ln_g, ln_b, w_gu, w_down, w_in, lam_qk, subln_g, rw_mu, rw_w0, rw_w2, rw_a0, rw_a2, rw_g2, rw_k_k, rw_k_a, rw_r_k, rw_lnx_g, rw_lnx_b, w_br_a, w_br_b, w_gate, b_gate, w_o):
    depth = w_ada.shape[0]
    b_p = x_prompt.shape[0]
    alpha = (2 * depth) ** 0.25
    mod = _ada_call(jnp.concatenate([c_prompt, c_sample], axis=0), w_ada, b_ada)
    mod = mod.reshape(depth, -1, 9, D_MODEL)

    w_gu_b, w_down_b, w_in_b = w_gu.astype(BF16), w_down.astype(BF16), w_in.astype(BF16)
    w_gate_b, w_br_a_b, w_br_b_b, w_o_b = (w.astype(BF16) for w in (w_gate, w_br_a, w_br_b, w_o))
    head_of = jnp.arange(W_B, dtype=jnp.int32) // DH_B
    bd = (head_of[:, None] == head_of[None, :]).astype(F32)
    zeros_l = jnp.zeros((depth, W_LORA, W_B), F32)
    rw_wwa = jnp.concatenate([jnp.concatenate([rw_w2, zeros_l], axis=2),
                              jnp.concatenate([zeros_l, rw_a2], axis=2)], axis=1)
    rw_vecs = jnp.stack([rw_w0, rw_a0, rw_k_k, rw_k_a, rw_r_k.reshape(depth, W_B), rw_lnx_g, rw_lnx_b,
                         jnp.zeros_like(rw_w0)], axis=1)

    zero_wkv = jnp.zeros((b_p, H_B, DH_B, DH_B), F32)
    zero_shift = jnp.zeros((b_p, N_RWKV), F32)
    hp, hd = x_prompt, x_sample
    outs_p, outs_d = [], []
    for l in range(depth):
        lw = {"w_gu0": w_gu_b[l, 0], "w_gu1": w_gu_b[l, 1], "w_down0": w_down_b[l, 0], "w_down1": w_down_b[l, 1],
              "w_in": w_in_b[l], "ln_g": ln_g[l], "ln_b": ln_b[l], "lam_qk": lam_qk[l], "subln_g": subln_g[l],
              "rw_mu": rw_mu[l], "rw_wwa": rw_wwa[l], "rw_g2": rw_g2[l], "rw_vecs": rw_vecs[l], "bd": bd,
              "w_gate": w_gate_b[l], "b_gate": b_gate[l], "w_br_a": w_br_a_b[l], "w_br_b": w_br_b_b[l],
              "w_o": w_o_b[l]}
        lam_init = 0.8 - 0.6 * math.exp(-0.3 * l)
        hp, *op = _layer(hp, mod[l, :b_p], lw, rel_bias, None, None, zero_wkv, zero_shift, lam_init, alpha)
        hd, *od = _layer(hd, mod[l, b_p:], lw, rel_bias, cache_k[l], cache_v[l], state_wkv[l], state_shift[l],
                         lam_init, alpha)
        outs_p.append(op)
        outs_d.append(od)
    stack = lambda outs, j: jnp.stack([o[j] for o in outs])
    return (hp, hd, stack(outs_p, 0), stack(outs_p, 1), stack(outs_p, 2), stack(outs_p, 3),
            stack(outs_d, 0), stack(outs_d, 1), stack(outs_d, 2), stack(outs_d, 3))
```

```python
import functools
import math

import jax
import jax.numpy as jnp
from jax import lax
from jax.experimental import pallas as pl
from jax.experimental.pallas import tpu as pltpu

F32 = jnp.float32
BF16 = jnp.bfloat16

D_MODEL = 1024
CHUNK = 64
H_A = 4
DH_A = 64
DV_A = 128
N_QK = 512
N_ATT = 1536
H_B = 8
DH_B = 64
W_B = 512
W_LORA = 64
A_LORA = 64
G_LORA = 128
N_RWKV = 1792
D_FF = 2816
N_BUCKETS = 32
LN_EPS = 1e-5
GN_EPS = 64e-5

LANES = 128
VMEM_LIMIT = 56 * 1024 * 1024
NEG_BIG = -1e30
FAR_BUCKET = N_BUCKETS // 2 - 1
BUCKET_STARTS = (12, 16, 23, 32, 46, 64, 91)

ATTN_TILE = 512
SOLVE_BLOCK = 16


def _params(*sem):
    return pltpu.CompilerParams(dimension_semantics=sem, vmem_limit_bytes=VMEM_LIMIT)


def _norm(x):
    mu = jnp.mean(x, axis=-1, keepdims=True)
    xc = x - mu
    var = jnp.mean(xc * xc, axis=-1, keepdims=True)
    return xc * lax.rsqrt(var + LN_EPS)


def _sigmoid(x):
    return 1.0 / (1.0 + jnp.exp(-x))


def _div_pow2(x, n):
    assert n & (n - 1) == 0
    return lax.shift_right_logical(x, n.bit_length() - 1)


def _mod_rows(mod_ref, sub):
    return tuple(mod_ref[:, 3 * sub + j:3 * sub + j + 1, :] for j in range(3))


def _ada_kernel(c_ref, w_ref, b_ref, o_ref):
    c = c_ref[...]
    sc = (c * _sigmoid(c)).astype(BF16)
    o_ref[0] = jnp.dot(sc, w_ref[0].astype(BF16), preferred_element_type=F32) + b_ref[0]


def _ada_call(c, w_ada, b_ada):
    depth, d, n = w_ada.shape
    nseq = c.shape[0]
    tn = 1152
    return pl.pallas_call(
        _ada_kernel,
        grid=(depth, n // tn),
        in_specs=[pl.BlockSpec((nseq, d), lambda l, j: (0, 0)),
                  pl.BlockSpec((1, d, tn), lambda l, j: (l, 0, j)),
                  pl.BlockSpec((1, 1, tn), lambda l, j: (l, 0, j))],
        out_specs=pl.BlockSpec((1, nseq, tn), lambda l, j: (l, 0, j)),
        out_shape=jax.ShapeDtypeStruct((depth, nseq, n), F32),
        compiler_params=_params("parallel", "parallel"),
        name="ada_mod",
    )(c, w_ada, b_ada.reshape(depth, 1, n))


def _ffn_kernel(x_ref, mod_ref, wgu_ref, wd_ref, g_ref, b_ref, o_ref, *, sub, fc, alpha):
    g_, ts, d = x_ref.shape
    x = x_ref[...]
    shift, scale, gate = _mod_rows(mod_ref, sub)
    u = (_norm(x) * (1.0 + scale) + shift).reshape(g_ * ts, d).astype(BF16)
    f = jnp.zeros((g_ * ts, d), F32)
    for c in range(D_FF // fc):
        hg = jnp.dot(u, wgu_ref[:, c * fc:(c + 1) * fc], preferred_element_type=F32)
        hu = jnp.dot(u, wgu_ref[:, D_FF + c * fc:D_FF + (c + 1) * fc], preferred_element_type=F32)
        act = (hg * _sigmoid(hg) * hu).astype(BF16)
        f = f + jnp.dot(act, wd_ref[c * fc:(c + 1) * fc, :], preferred_element_type=F32)
    y = alpha * x + (0.5 * gate) * f.reshape(g_, ts, d)
    o_ref[...] = _norm(y) * g_ref[...] + b_ref[...]


def _seq_tiling(nseq, s):
    if s >= 512:
        return 1, 512
    g = max(1, min(nseq, 256 // s))
    while nseq % g:
        g -= 1
    return g, s


def _ffn_call(x, mod, wgu, wd, ln_g, ln_b, sub, alpha):
    nseq, s, d = x.shape
    g_, ts = _seq_tiling(nseq, s)
    const = lambda i, j: (0, 0)
    return pl.pallas_call(
        functools.partial(_ffn_kernel, sub=sub, fc=D_FF // 2, alpha=alpha),
        grid=(nseq // g_, s // ts),
        in_specs=[pl.BlockSpec((g_, ts, d), lambda i, j: (i, j, 0)),
                  pl.BlockSpec((g_, 9, d), lambda i, j: (i, 0, 0)),
                  pl.BlockSpec(wgu.shape, const, pipeline_mode=pl.Buffered(1)),
                  pl.BlockSpec(wd.shape, const, pipeline_mode=pl.Buffered(1)),
                  pl.BlockSpec((1, d), const),
                  pl.BlockSpec((1, d), const)],
        out_specs=pl.BlockSpec((g_, ts, d), lambda i, j: (i, j, 0)),
        out_shape=jax.ShapeDtypeStruct(x.shape, F32),
        compiler_params=_params("parallel", "parallel"),
        name="ffn_block",
    )(x, mod, wgu, wd, ln_g.reshape(1, d), ln_b.reshape(1, d))


def _mixin_kernel(x_ref, mod_ref, win_ref, q_ref, k_ref, v_ref, kb_ref, vb_ref, prw_ref):
    g_, ts, d = x_ref.shape
    shift, scale, _ = _mod_rows(mod_ref, 1)
    u = (_norm(x_ref[...]) * (1.0 + scale) + shift).reshape(g_ * ts, d).astype(BF16)

    def proj(lo, hi):
        return jnp.dot(u, win_ref[:, lo:hi], preferred_element_type=F32).reshape(g_, ts, hi - lo)

    q = proj(0, N_QK)
    q_ref[...] = (q * (DH_A ** -0.5)).astype(BF16)
    k = proj(N_QK, 2 * N_QK)
    k_ref[...] = k
    kb_ref[...] = k.astype(BF16)
    v = proj(2 * N_QK, N_ATT)
    v_ref[...] = v
    vb_ref[...] = v.astype(BF16)
    prw_ref[...] = proj(N_ATT, N_ATT + N_RWKV)


def _mixin_call(x, mod, win):
    nseq, s, d = x.shape
    g_, ts = _seq_tiling(nseq, s)
    const = lambda i, j: (0, 0)
    blk = lambda w: pl.BlockSpec((g_, ts, w), lambda i, j: (i, j, 0))
    sds = lambda w, dt: jax.ShapeDtypeStruct((nseq, s, w), dt)
    return pl.pallas_call(
        _mixin_kernel,
        grid=(nseq // g_, s // ts),
        in_specs=[blk(d),
                  pl.BlockSpec((g_, 9, d), lambda i, j: (i, 0, 0)),
                  pl.BlockSpec(win.shape, const, pipeline_mode=pl.Buffered(1))],
        out_specs=[blk(N_QK), blk(N_QK), blk(N_QK), blk(N_QK), blk(N_QK), blk(N_RWKV)],
        out_shape=[sds(N_QK, BF16), sds(N_QK, F32), sds(N_QK, F32), sds(N_QK, BF16), sds(N_QK, BF16),
                   sds(N_RWKV, F32)],
        compiler_params=_params("parallel", "parallel"),
        name="mixer_in",
    )(x, mod, win)


def _bias_tile(rb_ref, h, tq, tk, offset, valid_cols):
    r = lax.broadcasted_iota(jnp.int32, (tq, tk), 0)
    c = lax.broadcasted_iota(jnp.int32, (tq, tk), 1)
    rel = c - r + offset
    n = jnp.abs(rel)
    log_bucket = jnp.full((tq, tk), N_BUCKETS // 4, jnp.int32)
    for start in BUCKET_STARTS:
        log_bucket = log_bucket + jnp.where(n >= start, 1, 0)
    bucket = jnp.where(rel > 0, N_BUCKETS // 2, 0) + jnp.where(n < N_BUCKETS // 4, n, log_bucket)
    bias = jnp.zeros((tq, tk), F32)
    for b in range(N_BUCKETS):
        bias = jnp.where(bucket == b, rb_ref[b * H_A + h], bias)
    if offset == 0:
        visible = jnp.logical_and(_div_pow2(c, CHUNK) <= _div_pow2(r, CHUNK), c < valid_cols)
        bias = jnp.where(visible, bias, NEG_BIG)
    return bias


def _attn_kernel(rb_ref, q_ref, k_ref, v_ref, lam_ref, g_ref, o_ref, m_scr, l_scr, acc_scr, bias_scr,
                 *, tq, tk, first_tile, valid_cols, lam_init):
    h = pl.program_id(1)
    i = pl.program_id(2)

    @pl.when(i == 0)
    def _():
        for slot, offset in ((0, -tk), (1, 0)):
            bias_scr[slot] = _bias_tile(rb_ref, h, tq, tk, offset, valid_cols)

    q = q_ref[0]
    lane = lax.broadcasted_iota(jnp.int32, (tq, LANES), 1)
    zero = jnp.zeros_like(q)
    q2 = jnp.concatenate([jnp.where(lane < DH_A, q, zero), jnp.where(lane < DH_A, zero, q)], axis=0)

    m_scr[...] = jnp.full(m_scr.shape, NEG_BIG, F32)
    l_scr[...] = jnp.zeros(l_scr.shape, F32)
    acc_scr[...] = jnp.zeros(acc_scr.shape, F32)

    def tile(j, bias):
        start = pl.multiple_of(j * tk, tk)
        kt = k_ref[0, pl.ds(start, tk), :]
        vt = v_ref[0, pl.ds(start, tk), :]
        s = lax.dot_general(q2, kt, (((1,), (1,)), ((), ())), preferred_element_type=F32)
        s = (s.reshape(2, tq, tk) + bias).reshape(2 * tq, tk)
        m_prev = m_scr[...]
        m_new = jnp.maximum(m_prev, jnp.max(s, axis=1, keepdims=True))
        p = jnp.exp(s - jnp.tile(m_new, (1, tk // LANES)))
        alpha = jnp.exp(m_prev - m_new)
        l_scr[...] = alpha * l_scr[...] + jnp.sum(p, axis=1, keepdims=True)
        acc_scr[...] = alpha * acc_scr[...] + jnp.dot(p.astype(BF16), vt, preferred_element_type=F32)
        m_scr[...] = m_new

    last = first_tile + i
    far_bias = rb_ref[FAR_BUCKET * H_A + h]

    def far_body(j, carry):
        tile(j, far_bias)
        return carry

    lax.fori_loop(0, last - 1, far_body, 0)

    @pl.when(last >= 1)
    def _():
        tile(last - 1, bias_scr[0])

    tile(last, bias_scr[1])

    lq = lam_ref[...]
    lam = (jnp.exp(jnp.sum(lq[0:1] * lq[1:2], axis=1, keepdims=True))
           - jnp.exp(jnp.sum(lq[2:3] * lq[3:4], axis=1, keepdims=True)) + lam_init)
    o_all = acc_scr[...] / l_scr[...]
    o = o_all[:tq] - lam * o_all[tq:]
    o = o * lax.rsqrt(jnp.mean(o * o, axis=-1, keepdims=True) + LN_EPS)
    o_ref[0] = (o * g_ref[...] * (1.0 - lam_init)).astype(o_ref.dtype)


def _attn_call(q, k, v, rel_bias, lam_qk, subln_g, q_off, sk_valid, lam_init):
    b, sq, _ = q.shape
    sk = k.shape[1]
    tk = ATTN_TILE
    tq = min(tk, sq)
    nq = sq // tq
    n_tiles = sk // tk
    valid_cols = sk_valid - (n_tiles - 1) * tk
    assert sq % tq == 0 and sk % tk == 0 and q_off % tk == 0 and 0 < valid_cols <= tk
    assert (nq == 1 or tq == tk) and (valid_cols == tk or nq == 1)
    assert tq % CHUNK == 0 or nq == 1
    assert q_off // tk + nq == n_tiles
    kern = functools.partial(_attn_kernel, tq=tq, tk=tk, first_tile=q_off // tk, valid_cols=valid_cols,
                             lam_init=lam_init)
    return pl.pallas_call(
        kern,
        grid=(b, H_A, nq),
        in_specs=[pl.BlockSpec(memory_space=pltpu.SMEM),
                  pl.BlockSpec((1, tq, LANES), lambda b_, h, i: (b_, i, h)),
                  pl.BlockSpec((1, sk, LANES), lambda b_, h, i: (b_, 0, h)),
                  pl.BlockSpec((1, sk, LANES), lambda b_, h, i: (b_, 0, h)),
                  pl.BlockSpec((4, DH_A), lambda b_, h, i: (0, 0)),
                  pl.BlockSpec((1, DV_A), lambda b_, h, i: (0, 0))],
        out_specs=pl.BlockSpec((1, tq, LANES), lambda b_, h, i: (b_, i, h)),
        out_shape=jax.ShapeDtypeStruct((b, sq, H_A * DV_A), BF16),
        scratch_shapes=[pltpu.VMEM((2 * tq, LANES), F32), pltpu.VMEM((2 * tq, LANES), F32),
                        pltpu.VMEM((2 * tq, DV_A), F32), pltpu.VMEM((2, tq, tk), F32)],
        compiler_params=_params("parallel", "parallel", "arbitrary"),
        name="diff_attn",
    )(rel_bias.reshape(-1), q, k, v, lam_qk, subln_g.reshape(1, DV_A))


_NN = (((1,), (0,)), ((), ()))
_NT = (((1,), (1,)), ((), ()))
_TN = (((0,), (0,)), ((), ()))


def _split(x):
    hi = x.astype(BF16)
    return hi, (x - hi.astype(F32)).astype(BF16)


def _cat(pairs, axis):
    return tuple(jnp.concatenate(parts, axis=axis) for parts in zip(*pairs))


def _pdot(a, b, dims=_NN):
    (ah, al), (bh, bl) = a, b
    ca, cb = dims[0][0][0], dims[0][1][0]
    two = lax.dot_general(jnp.concatenate([ah, al], axis=ca), jnp.concatenate([bh, bh], axis=cb), dims,
                          preferred_element_type=F32)
    return two + lax.dot_general(ah, bl, dims, preferred_element_type=F32)


def _pdot_ones_rhs(a, ones):
    return jnp.dot(jnp.concatenate(a, axis=1), jnp.concatenate([ones, ones], axis=0),
                   preferred_element_type=F32)


def _pdot_ones_lhs(ones, b):
    return jnp.dot(jnp.concatenate([ones, ones], axis=1), jnp.concatenate(b, axis=0),
                   preferred_element_type=F32)


def _rwkv_kernel(prw_ref, shift_ref, s0_ref, mu_ref, wwa_hi_ref, wwa_lo_ref, g2_hi_ref, g2_lo_ref, vec_ref,
                 bd_ref, y_ref, sout_ref, carry_scr, state_scr, *, c_len):
    ci = pl.program_id(1)
    nc = pl.num_programs(1)
    n_pairs = H_B // 2

    @pl.when(ci == 0)
    def _():
        carry_scr[...] = shift_ref[0]
        state_scr[...] = s0_ref[0]

    p = prw_ref[0]
    row = lax.broadcasted_iota(jnp.int32, p.shape, 0)
    prev = jnp.where(row == 0, carry_scr[...], pltpu.roll(p, 1, axis=0))
    carry_scr[...] = p[c_len - 1:c_len]
    xm = p + (prev - p) * mu_ref[...]

    w0, a0, k_k, k_a, r_k, lnx_g, lnx_b = (vec_ref[j:j + 1] for j in range(7))
    r = xm[:, :W_B]
    k = xm[:, W_B:2 * W_B]
    v = xm[:, 2 * W_B:3 * W_B]
    lora_in = xm[:, 3 * W_B:3 * W_B + W_LORA + A_LORA]
    lane = lax.broadcasted_iota(jnp.int32, lora_in.shape, 1)
    lora = _pdot(_split(jnp.where(lane < W_LORA, jnp.tanh(lora_in), lora_in)),
                 (wwa_hi_ref[...], wwa_lo_ref[...]))
    ww = -(w0 + lora[:, :W_B])
    w_log = -(jnp.maximum(ww, 0.0) + jnp.log(1.0 + jnp.exp(-jnp.abs(ww)))) - 0.5
    lw = -jnp.exp(w_log)
    a = _sigmoid(a0 + lora[:, W_B:])
    g = _pdot(_split(_sigmoid(xm[:, 3 * W_B + W_LORA + A_LORA:])), (g2_hi_ref[...], g2_lo_ref[...]))

    bd = bd_ref[...]

    def head_sum(x):
        return jnp.concatenate([_pdot_ones_rhs(_split(x[:, j * LANES:(j + 1) * LANES]), bd)
                                for j in range(W_B // LANES)], axis=1)

    kk = k * k_k
    kk = kk * lax.rsqrt(jnp.maximum(head_sum(kk * kk), 1e-24))
    km = k * (1.0 + (a - 1.0) * k_a)
    kb = kk * a
    bonus = head_sum(r * km * r_k) * v

    c2 = 2 * c_len
    tri_r = lax.broadcasted_iota(jnp.int32, (c_len, c_len), 0)
    tri_c = lax.broadcasted_iota(jnp.int32, (c_len, c_len), 1)
    tril = jnp.where(tri_r >= tri_c, 1.0, 0.0).astype(BF16)
    rr = lax.broadcasted_iota(jnp.int32, (c2, c2), 0)
    cc = lax.broadcasted_iota(jnp.int32, (c2, c2), 1)
    strict = rr > cc
    incl = rr >= cc
    eye = rr == cc
    same_blk = _div_pow2(rr, SOLVE_BLOCK) == _div_pow2(cc, SOLVE_BLOCK)
    head0 = lax.broadcasted_iota(jnp.int32, (c_len, LANES), 1) < DH_B

    def stack(x):
        z = jnp.zeros_like(x)
        return jnp.concatenate([jnp.where(head0, x, z), jnp.where(head0, z, x)], axis=0)

    def plus_eye(x):
        return jnp.where(eye, x + 1.0, x)

    def par(fn, *per_pair):
        return [fn(*args) for args in zip(*per_pair)]

    pairs = [slice(j * LANES, (j + 1) * LANES) for j in range(n_pairs)]
    zero = jnp.zeros((c2, c2), F32)
    lw_p = [lw[:, sl] for sl in pairs]
    cum = par(lambda x: _pdot_ones_lhs(tril, _split(x)), lw_p)
    g_in = par(jnp.exp, cum)
    g_inv = par(lambda x: jnp.exp(-x), cum)
    a_hat = par(lambda sl, x, y: _split(stack(-kk[:, sl] * jnp.exp(x - y))), pairs, cum, lw_p)
    r_hat = par(lambda sl, x: _split(stack(r[:, sl] * x)), pairs, g_in)
    b_til = par(lambda sl, x: _split(stack(kb[:, sl] * x)), pairs, g_inv)
    k_til = par(lambda sl, x: _split(stack(km[:, sl] * x)), pairs, g_inv)
    v2 = par(lambda sl: _split(stack(v[:, sl])), pairs)
    lhs = par(lambda x, y: _cat([x, y], 0), a_hat, r_hat)
    ab_rb = par(lambda x, y: _pdot(x, y, _NT), lhs, b_til)
    ak_rk = par(lambda x, y: _pdot(x, y, _NT), lhs, k_til)
    a_ab = par(lambda x: jnp.where(strict, x[:c2], zero), ab_rb)
    a_rb = par(lambda x: _split(jnp.where(incl, x[c2:], zero)), ab_rb)
    a_ak = par(lambda x: _split(jnp.where(strict, x[:c2], zero)), ak_rk)
    a_rk = par(lambda x: _split(jnp.where(incl, x[c2:], zero)), ak_rk)

    d1 = par(lambda x: jnp.where(same_blk, x, zero), a_ab)
    dinv = par(plus_eye, d1)
    dpow = par(_split, d1)
    for _ in range(int(math.log2(SOLVE_BLOCK)) - 1):
        dpow = par(lambda x: _split(_pdot(x, x)), dpow)
        dinv = par(lambda x, y: x + _pdot(_split(x), y), dinv, dpow)
    npow = par(lambda x, y, z: _split(_pdot(_split(x), _split(y - z))), dinv, a_ab, d1)
    t_mat = dinv
    n_levels = max(1, int(math.ceil(math.log2(c_len // SOLVE_BLOCK))))
    for level in range(n_levels):
        t_mat = par(lambda x, y: y + _pdot(x, _split(y)), npow, t_mat)
        if level + 1 < n_levels:
            npow = par(lambda x: _split(_pdot(x, x)), npow)
    t_mat = par(_split, t_mat)

    s_pair = [state_scr[j] for j in range(n_pairs)]
    s_split = par(_split, s_pair)
    rhs = par(lambda x, y, z, w: _split(_pdot(x, y, _NT) + _pdot(z, w)), a_hat, s_split, a_ak, v2)
    u = par(lambda x, y: _split(_pdot(x, y)), t_mat, rhs)
    o2 = par(lambda x, y, z, w, p_, q_: _pdot(x, y, _NT) + _pdot(z, w) + _pdot(p_, q_),
             r_hat, s_split, a_rk, v2, a_rb, u)
    outs = par(lambda x: x[:c_len] + x[c_len:], o2)
    s_add = par(lambda x, y, z, w: _pdot(_cat([x, y], 0), _cat([z, w], 0), _TN), v2, u, k_til, b_til)
    for j in range(n_pairs):
        state_scr[j] = (s_pair[j] + s_add[j]) * g_in[j][c_len - 1:c_len]

    o = jnp.concatenate(outs, axis=1)
    inv_dh = 1.0 / DH_B
    mean = head_sum(o) * inv_dh
    oc = o - mean
    var = head_sum(oc * oc) * inv_dh
    on = oc * lax.rsqrt(var + GN_EPS) * lnx_g + lnx_b
    y_ref[0] = ((on + bonus) * g).astype(y_ref.dtype)

    @pl.when(ci == nc - 1)
    def _():
        sout_ref[0] = state_scr[...]


def _pair_states(wkv):
    b = wkv.shape[0]
    w = wkv.reshape(b, H_B // 2, 2, DH_B, DH_B)
    z = jnp.zeros_like(w[:, :, 0])
    top = jnp.concatenate([w[:, :, 0], z], axis=-1)
    bot = jnp.concatenate([z, w[:, :, 1]], axis=-1)
    return jnp.concatenate([top, bot], axis=-2)


def _unpair_states(sp):
    b = sp.shape[0]
    return jnp.stack([sp[:, :, :DH_B, :DH_B], sp[:, :, DH_B:, DH_B:]], axis=2).reshape(b, H_B, DH_B, DH_B)


def _rwkv_call(prw, shift, wkv, mu, wwa, g2, vecs):
    nseq, s, _ = prw.shape
    c_len = min(CHUNK, s)
    assert s % c_len == 0 and c_len % SOLVE_BLOCK == 0
    const2 = lambda b_, c: (0, 0)
    head_of = jnp.arange(LANES, dtype=jnp.int32) // DH_B
    bd = (head_of[:, None] == head_of[None, :]).astype(BF16)
    wwa_hi, wwa_lo = _split(wwa)
    g2_hi, g2_lo = _split(g2)
    y, s_out = pl.pallas_call(
        functools.partial(_rwkv_kernel, c_len=c_len),
        grid=(nseq, s // c_len),
        in_specs=[pl.BlockSpec((1, c_len, N_RWKV), lambda b_, c: (b_, c, 0)),
                  pl.BlockSpec((1, 1, N_RWKV), lambda b_, c: (b_, 0, 0)),
                  pl.BlockSpec((1, H_B // 2, LANES, LANES), lambda b_, c: (b_, 0, 0, 0)),
                  pl.BlockSpec((1, N_RWKV), const2),
                  pl.BlockSpec(wwa.shape, const2),
                  pl.BlockSpec(wwa.shape, const2),
                  pl.BlockSpec(g2.shape, const2),
                  pl.BlockSpec(g2.shape, const2),
                  pl.BlockSpec(vecs.shape, const2),
                  pl.BlockSpec(bd.shape, const2)],
        out_specs=[pl.BlockSpec((1, c_len, W_B), lambda b_, c: (b_, c, 0)),
                   pl.BlockSpec((1, H_B // 2, LANES, LANES), lambda b_, c: (b_, 0, 0, 0))],
        out_shape=[jax.ShapeDtypeStruct((nseq, s, W_B), BF16),
                   jax.ShapeDtypeStruct((nseq, H_B // 2, LANES, LANES), F32)],
        scratch_shapes=[pltpu.VMEM((1, N_RWKV), F32), pltpu.VMEM((H_B // 2, LANES, LANES), F32)],
        compiler_params=_params("parallel", "arbitrary"),
        name="rwkv7_chunk",
    )(prw, shift.reshape(nseq, 1, N_RWKV), _pair_states(wkv), mu.reshape(1, N_RWKV), wwa_hi, wwa_lo, g2_hi, g2_lo,
      vecs, bd)
    return y, _unpair_states(s_out)


def _merge_kernel(x_ref, mod_ref, ya_ref, yb_ref, wg_ref, bg_ref, wa_ref, wb_ref, wo_ref, g_ref, b_ref,
                  o_ref, *, alpha):
    g_, ts, d = x_ref.shape
    m = g_ * ts
    x = x_ref[...]
    shift, scale, gate = _mod_rows(mod_ref, 1)
    u = (_norm(x) * (1.0 + scale) + shift).reshape(m, d).astype(BF16)
    ma = jnp.dot(ya_ref[...].reshape(m, -1), wa_ref[...], preferred_element_type=F32)
    mb = jnp.dot(yb_ref[...].reshape(m, -1), wb_ref[...], preferred_element_type=F32)
    ga = _sigmoid(jnp.dot(u, wg_ref[:, :d], preferred_element_type=F32) + bg_ref[:, :d])
    gb = _sigmoid(jnp.dot(u, wg_ref[:, d:], preferred_element_type=F32) + bg_ref[:, d:])
    merged = (ga * ma + gb * mb).astype(BF16)
    z = jnp.dot(merged, wo_ref[...], preferred_element_type=F32).reshape(g_, ts, d)
    o_ref[...] = _norm(alpha * x + gate * z) * g_ref[...] + b_ref[...]


def _merge_call(x, mod, ya, yb, wg, bg, wa, wb, wo, ln_g, ln_b, alpha):
    nseq, s, d = x.shape
    g_, ts = _seq_tiling(nseq, s)
    const = lambda i, j: (0, 0)
    blk = lambda w: pl.BlockSpec((g_, ts, w), lambda i, j: (i, j, 0))
    full = lambda a: pl.BlockSpec(a.shape, const, pipeline_mode=pl.Buffered(1))
    return pl.pallas_call(
        functools.partial(_merge_kernel, alpha=alpha),
        grid=(nseq // g_, s // ts),
        in_specs=[blk(d), pl.BlockSpec((g_, 9, d), lambda i, j: (i, 0, 0)), blk(ya.shape[-1]), blk(yb.shape[-1]),
                  full(wg), pl.BlockSpec((1, 2 * d), const), full(wa), full(wb), full(wo),
                  pl.BlockSpec((1, d), const), pl.BlockSpec((1, d), const)],
        out_specs=blk(d),
        out_shape=jax.ShapeDtypeStruct(x.shape, F32),
        compiler_params=_params("parallel", "parallel"),
        name="merge_out",
    )(x, mod, ya, yb, wg, bg.reshape(1, 2 * d), wa, wb, wo, ln_g.reshape(1, d), ln_b.reshape(1, d))


def _layer(x, mod, lw, rel_bias, past_k, past_v, wkv, shift, lam_init, alpha):
    nseq, s, _ = x.shape
    x = _ffn_call(x, mod, lw["w_gu0"], lw["w_down0"], lw["ln_g"][0], lw["ln_b"][0], 0, alpha)
    q, k_new, v_new, kb, vb, prw = _mixin_call(x, mod, lw["w_in"])
    if past_k is None:
        k_all, v_all, q_off, sk_valid = kb, vb, 0, s
    else:
        past = past_k.shape[1]
        sk_valid = past + s
        pad = (-sk_valid) % ATTN_TILE
        zeros = jnp.zeros((nseq, pad, N_QK), BF16)
        k_all = jnp.concatenate([past_k.reshape(nseq, past, N_QK).astype(BF16), kb, zeros], axis=1)
        v_all = jnp.concatenate([past_v.reshape(nseq, past, N_QK).astype(BF16), vb, zeros], axis=1)
        q_off = past
    ya = _attn_call(q, k_all, v_all, rel_bias, lw["lam_qk"], lw["subln_g"], q_off, sk_valid, lam_init)
    yb, wkv_new = _rwkv_call(prw, shift, wkv, lw["rw_mu"], lw["rw_wwa"], lw["rw_g2"], lw["rw_vecs"])
    x = _merge_call(x, mod, ya, yb, lw["w_gate"], lw["b_gate"], lw["w_br_a"], lw["w_br_b"], lw["w_o"],
                    lw["ln_g"][1], lw["ln_b"][1], alpha)
    x = _ffn_call(x, mod, lw["w_gu1"], lw["w_down1"], lw["ln_g"][2], lw["ln_b"][2], 2, alpha)
    return (x, k_new.reshape(nseq, s, H_A, 2 * DH_A), v_new.reshape(nseq, s, H_A, DV_A), wkv_new,
            prw[:, -1])


def kernel(x_prompt, x_sample, c_prompt, c_sample, cache_k, cache_v, state_wkv, state_shift, rel_bias, w_ada, b_ada, ln_g, ln_b, w_gu, w_down, w_in, lam_qk, subln_g, rw_mu, rw_w0, rw_w2, rw_a0, rw_a2, rw_g2, rw_k_k, rw_k_a, rw_r_k, rw_lnx_g, rw_lnx_b, w_br_a, w_br_b, w_gate, b_gate, w_o):
    depth = w_ada.shape[0]
    b_p = x_prompt.shape[0]
    alpha = (2 * depth) ** 0.25
    mod = _ada_call(jnp.concatenate([c_prompt, c_sample], axis=0), w_ada, b_ada)
    mod = mod.reshape(depth, -1, 9, D_MODEL)

    w_gu_b, w_down_b, w_in_b = w_gu.astype(BF16), w_down.astype(BF16), w_in.astype(BF16)
    w_gate_b, w_br_a_b, w_br_b_b, w_o_b = (w.astype(BF16) for w in (w_gate, w_br_a, w_br_b, w_o))
    zeros_l = jnp.zeros((depth, W_LORA, W_B), F32)
    rw_wwa = jnp.concatenate([jnp.concatenate([rw_w2, zeros_l], axis=2),
                              jnp.concatenate([zeros_l, rw_a2], axis=2)], axis=1)
    rw_vecs = jnp.stack([rw_w0, rw_a0, rw_k_k, rw_k_a, rw_r_k.reshape(depth, W_B), rw_lnx_g, rw_lnx_b,
                         jnp.zeros_like(rw_w0)], axis=1)

    zero_wkv = jnp.zeros((b_p, H_B, DH_B, DH_B), F32)
    zero_shift = jnp.zeros((b_p, N_RWKV), F32)
    hp, hd = x_prompt, x_sample
    outs_p, outs_d = [], []
    for l in range(depth):
        lw = {"w_gu0": w_gu_b[l, 0], "w_gu1": w_gu_b[l, 1], "w_down0": w_down_b[l, 0], "w_down1": w_down_b[l, 1],
              "w_in": w_in_b[l], "ln_g": ln_g[l], "ln_b": ln_b[l], "lam_qk": lam_qk[l], "subln_g": subln_g[l],
              "rw_mu": rw_mu[l], "rw_wwa": rw_wwa[l], "rw_g2": rw_g2[l], "rw_vecs": rw_vecs[l],
              "w_gate": w_gate_b[l], "b_gate": b_gate[l], "w_br_a": w_br_a_b[l], "w_br_b": w_br_b_b[l],
              "w_o": w_o_b[l]}
        lam_init = 0.8 - 0.6 * math.exp(-0.3 * l)
        hp, *op = _layer(hp, mod[l, :b_p], lw, rel_bias, None, None, zero_wkv, zero_shift, lam_init, alpha)
        hd, *od = _layer(hd, mod[l, b_p:], lw, rel_bias, cache_k[l], cache_v[l], state_wkv[l], state_shift[l],
                         lam_init, alpha)
        outs_p.append(op)
        outs_d.append(od)
    stack = lambda outs, j: jnp.stack([o[j] for o in outs])
    return (hp, hd, stack(outs_p, 0), stack(outs_p, 1), stack(outs_p, 2), stack(outs_p, 3),
            stack(outs_d, 0), stack(outs_d, 1), stack(outs_d, 2), stack(outs_d, 3))
```

```python
import functools
import math

import jax
import jax.numpy as jnp
from jax import lax
from jax.experimental import pallas as pl
from jax.experimental.pallas import tpu as pltpu

F32 = jnp.float32
BF16 = jnp.bfloat16

D_MODEL = 1024
CHUNK = 64
H_A = 4
DH_A = 64
DV_A = 128
N_QK = 512
N_ATT = 1536
H_B = 8
DH_B = 64
W_B = 512
W_LORA = 64
A_LORA = 64
G_LORA = 128
N_RWKV = 1792
D_FF = 2816
N_BUCKETS = 32
LN_EPS = 1e-5
GN_EPS = 64e-5

LANES = 128
VMEM_LIMIT = 56 * 1024 * 1024
NEG_BIG = -1e30
FAR_BUCKET = N_BUCKETS // 2 - 1
BUCKET_STARTS = (12, 16, 23, 32, 46, 64, 91)

ATTN_TILE = 512
ROW_GROUPS = 4
LOG2E = 1.4426950408889634
_NN = (((1,), (0,)), ((), ()))
_NT = (((1,), (1,)), ((), ()))
_TN = (((0,), (0,)), ((), ()))
SOLVE_BLOCK = 16


def _params(*sem):
    return pltpu.CompilerParams(dimension_semantics=sem, vmem_limit_bytes=VMEM_LIMIT)


def _norm(x):
    mu = jnp.mean(x, axis=-1, keepdims=True)
    xc = x - mu
    var = jnp.mean(xc * xc, axis=-1, keepdims=True)
    return xc * lax.rsqrt(var + LN_EPS)


def _sigmoid(x):
    return 1.0 / (1.0 + jnp.exp(-x))


def _div_pow2(x, n):
    assert n & (n - 1) == 0
    return lax.shift_right_logical(x, n.bit_length() - 1)


def _mod_rows(mod_ref, sub):
    return tuple(mod_ref[:, 3 * sub + j:3 * sub + j + 1, :] for j in range(3))


def _ada_kernel(c_ref, w_ref, b_ref, o_ref):
    c = c_ref[...]
    sc = (c * _sigmoid(c)).astype(BF16)
    o_ref[0] = jnp.dot(sc, w_ref[0].astype(BF16), preferred_element_type=F32) + b_ref[0]


def _ada_call(c, w_ada, b_ada):
    depth, d, n = w_ada.shape
    nseq = c.shape[0]
    tn = 1152
    return pl.pallas_call(
        _ada_kernel,
        grid=(depth, n // tn),
        in_specs=[pl.BlockSpec((nseq, d), lambda l, j: (0, 0)),
                  pl.BlockSpec((1, d, tn), lambda l, j: (l, 0, j)),
                  pl.BlockSpec((1, 1, tn), lambda l, j: (l, 0, j))],
        out_specs=pl.BlockSpec((1, nseq, tn), lambda l, j: (l, 0, j)),
        out_shape=jax.ShapeDtypeStruct((depth, nseq, n), F32),
        compiler_params=_params("parallel", "parallel"),
        name="ada_mod",
    )(c, w_ada, b_ada.reshape(depth, 1, n))


def _ffn_kernel(x_ref, mod_ref, wgu_ref, wd_ref, g_ref, b_ref, o_ref, *, sub, fc, alpha):
    g_, ts, d = x_ref.shape
    x = x_ref[...]
    shift, scale, gate = _mod_rows(mod_ref, sub)
    u = (_norm(x) * (1.0 + scale) + shift).reshape(g_ * ts, d).astype(BF16)
    f = jnp.zeros((g_ * ts, d), F32)
    for c in range(D_FF // fc):
        hg = jnp.dot(u, wgu_ref[:, c * fc:(c + 1) * fc], preferred_element_type=F32)
        hu = jnp.dot(u, wgu_ref[:, D_FF + c * fc:D_FF + (c + 1) * fc], preferred_element_type=F32)
        act = (hg * _sigmoid(hg) * hu).astype(BF16)
        f = f + jnp.dot(act, wd_ref[c * fc:(c + 1) * fc, :], preferred_element_type=F32)
    y = alpha * x + (0.5 * gate) * f.reshape(g_, ts, d)
    o_ref[...] = _norm(y) * g_ref[...] + b_ref[...]


def _seq_tiling(nseq, s):
    if s >= 512:
        return 1, 512
    g = max(1, min(nseq, 256 // s))
    while nseq % g:
        g -= 1
    return g, s


def _ffn_call(x, mod, wgu, wd, ln_g, ln_b, sub, alpha):
    nseq, s, d = x.shape
    g_, ts = _seq_tiling(nseq, s)
    const = lambda i, j: (0, 0)
    return pl.pallas_call(
        functools.partial(_ffn_kernel, sub=sub, fc=D_FF // 2, alpha=alpha),
        grid=(nseq // g_, s // ts),
        in_specs=[pl.BlockSpec((g_, ts, d), lambda i, j: (i, j, 0)),
                  pl.BlockSpec((g_, 9, d), lambda i, j: (i, 0, 0)),
                  pl.BlockSpec(wgu.shape, const, pipeline_mode=pl.Buffered(1)),
                  pl.BlockSpec(wd.shape, const, pipeline_mode=pl.Buffered(1)),
                  pl.BlockSpec((1, d), const),
                  pl.BlockSpec((1, d), const)],
        out_specs=pl.BlockSpec((g_, ts, d), lambda i, j: (i, j, 0)),
        out_shape=jax.ShapeDtypeStruct(x.shape, F32),
        compiler_params=_params("parallel", "parallel"),
        name="ffn_block",
    )(x, mod, wgu, wd, ln_g.reshape(1, d), ln_b.reshape(1, d))


def _mixin_kernel(x_ref, mod_ref, win_ref, *refs):
    q_ref, k_ref, v_ref, kb_ref, vb_ref, prw_ref = refs[-6:]
    g_, ts, d = x_ref.shape
    shift, scale, _ = _mod_rows(mod_ref, 1)
    u = (_norm(x_ref[...]) * (1.0 + scale) + shift).reshape(g_ * ts, d).astype(BF16)

    def proj(lo, hi):
        return jnp.dot(u, win_ref[:, lo:hi], preferred_element_type=F32).reshape(g_, ts, hi - lo)

    q = proj(0, N_QK)
    q_ref[...] = (q * (DH_A ** -0.5 * LOG2E)).astype(BF16)
    k = proj(N_QK, 2 * N_QK)
    kb_ref[...] = k.astype(BF16)
    v = proj(2 * N_QK, N_ATT)
    vb_ref[...] = v.astype(BF16)
    for h in range(H_A):
        k_ref[0, :, :, h, :] = k[:, :, h * LANES:(h + 1) * LANES]
        v_ref[0, :, :, h, :] = v[:, :, h * LANES:(h + 1) * LANES]
    prw_ref[...] = proj(N_ATT, N_ATT + N_RWKV)


def _mixin_call(x, mod, win, layer, depth, stacks):
    nseq, s, d = x.shape
    g_, ts = _seq_tiling(nseq, s)
    const = lambda i, j: (0, 0)
    blk = lambda w: pl.BlockSpec((g_, ts, w), lambda i, j: (i, j, 0))
    sds = lambda w, dt: jax.ShapeDtypeStruct((nseq, s, w), dt)
    slab = pl.BlockSpec((1, g_, ts, H_A, LANES), lambda i, j: (layer, i, j, 0, 0))
    stack_sds = jax.ShapeDtypeStruct((depth, nseq, s, H_A, LANES), F32)
    in_specs = [blk(d),
                pl.BlockSpec((g_, 9, d), lambda i, j: (i, 0, 0)),
                pl.BlockSpec(win.shape, const, pipeline_mode=pl.Buffered(1))]
    args, aliases = [x, mod, win], {}
    if stacks is not None:
        in_specs += [pl.BlockSpec(memory_space=pl.ANY)] * 2
        args += list(stacks)
        aliases = {3: 1, 4: 2}
    return pl.pallas_call(
        _mixin_kernel,
        grid=(nseq // g_, s // ts),
        in_specs=in_specs,
        out_specs=[blk(N_QK), slab, slab, blk(N_QK), blk(N_QK), blk(N_RWKV)],
        out_shape=[sds(N_QK, BF16), stack_sds, stack_sds, sds(N_QK, BF16), sds(N_QK, BF16), sds(N_RWKV, F32)],
        input_output_aliases=aliases,
        compiler_params=_params("parallel", "parallel"),
        name="mixer_in",
    )(*args)


def _bias_tile(rb_ref, h, tq, tk, offset, valid_cols):
    r = lax.broadcasted_iota(jnp.int32, (tq, tk), 0)
    c = lax.broadcasted_iota(jnp.int32, (tq, tk), 1)
    rel = c - r + offset
    n = jnp.abs(rel)
    log_bucket = jnp.full((tq, tk), N_BUCKETS // 4, jnp.int32)
    for start in BUCKET_STARTS:
        log_bucket = log_bucket + jnp.where(n >= start, 1, 0)
    bucket = jnp.where(rel > 0, N_BUCKETS // 2, 0) + jnp.where(n < N_BUCKETS // 4, n, log_bucket)
    bias = jnp.zeros((tq, tk), F32)
    for b in range(N_BUCKETS):
        bias = jnp.where(bucket == b, rb_ref[b * H_A + h], bias)
    if offset == 0:
        visible = jnp.logical_and(_div_pow2(c, CHUNK) <= _div_pow2(r, CHUNK), c < valid_cols)
        bias = jnp.where(visible, bias, NEG_BIG)
    return bias


def _stack_maps(q):
    lane = lax.broadcasted_iota(jnp.int32, q.shape, 1)
    zero = jnp.zeros_like(q)
    return jnp.concatenate([jnp.where(lane < DH_A, q, zero), jnp.where(lane < DH_A, zero, q)], axis=0)


def _diff_head_out(o_all, tq, lam_ref, g_ref, lam_init):
    lq = lam_ref[...]
    lam = (jnp.exp(jnp.sum(lq[0:1] * lq[1:2], axis=1, keepdims=True))
           - jnp.exp(jnp.sum(lq[2:3] * lq[3:4], axis=1, keepdims=True)) + lam_init)
    o = o_all[:tq] - lam * o_all[tq:]
    o = o * lax.rsqrt(jnp.mean(o * o, axis=-1, keepdims=True) + LN_EPS)
    return o * g_ref[...] * (1.0 - lam_init)


def _attn_kernel(rb_ref, q_ref, k_ref, v_ref, lam_ref, g_ref, o_ref, m_scr, acc_scr, bias_scr,
                 *, tq, tk, lam_init):
    h = pl.program_id(1)
    i = pl.program_id(2)

    @pl.when(i == 0)
    def _():
        for slot, offset in ((0, -tk), (1, 0)):
            bias_scr[slot] = _bias_tile(rb_ref, h, tq, tk, offset, tk) * LOG2E

    q2 = _stack_maps(q_ref[0])
    m_scr[...] = jnp.full(m_scr.shape, NEG_BIG, F32)
    acc_scr[...] = jnp.zeros(acc_scr.shape, F32)
    ones = jnp.ones((tk, LANES), BF16)

    def tile(j, bias, shift):
        start = pl.multiple_of(j * tk, tk)
        kt = k_ref[0, pl.ds(start, tk), :]
        v1 = jnp.concatenate([v_ref[0, pl.ds(start, tk), :], ones], axis=1)
        rows = 2 * tq // ROW_GROUPS
        scores = [lax.dot_general(q2[g * rows:(g + 1) * rows], kt, _NT, preferred_element_type=F32)
                  for g in range(ROW_GROUPS)]
        for g, s in enumerate(scores):
            sl = slice(g * rows, (g + 1) * rows)
            if bias is not None:
                r0 = (g * rows) % tq
                s = s + bias[r0:r0 + rows]
            m_prev = m_scr[sl]
            m_new = jnp.maximum(m_prev, jnp.max(s, axis=1, keepdims=True) + shift)
            p = jnp.exp2(s - jnp.tile(m_new - shift, (1, tk // LANES)))
            alpha = jnp.exp2(m_prev - m_new)
            acc_scr[sl] = (jnp.tile(alpha, (1, 2)) * acc_scr[sl]
                           + jnp.dot(p.astype(BF16), v1, preferred_element_type=F32))
            m_scr[sl] = m_new

    far_shift = rb_ref[FAR_BUCKET * H_A + h] * LOG2E

    def far_body(j, carry):
        tile(j, None, far_shift)
        return carry

    lax.fori_loop(0, i - 1, far_body, 0)

    @pl.when(i >= 1)
    def _():
        tile(i - 1, bias_scr[0], 0.0)

    tile(i, bias_scr[1], 0.0)

    acc = acc_scr[...]
    o_ref[0] = _diff_head_out(acc[:, :DV_A] / acc[:, DV_A:], tq, lam_ref, g_ref, lam_init).astype(o_ref.dtype)


def _attn_call(q, k, v, rel_bias, lam_qk, subln_g, lam_init):
    b, s, _ = q.shape
    tq = tk = ATTN_TILE
    assert s % tk == 0 and tk % CHUNK == 0 and (2 * tq) % ROW_GROUPS == 0 and tq % (2 * tq // ROW_GROUPS) == 0
    kern = functools.partial(_attn_kernel, tq=tq, tk=tk, lam_init=lam_init)
    return pl.pallas_call(
        kern,
        grid=(b, H_A, s // tq),
        in_specs=[pl.BlockSpec(memory_space=pltpu.SMEM),
                  pl.BlockSpec((1, tq, LANES), lambda b_, h, i: (b_, i, h)),
                  pl.BlockSpec((1, s, LANES), lambda b_, h, i: (b_, 0, h)),
                  pl.BlockSpec((1, s, LANES), lambda b_, h, i: (b_, 0, h)),
                  pl.BlockSpec((4, DH_A), lambda b_, h, i: (0, 0)),
                  pl.BlockSpec((1, DV_A), lambda b_, h, i: (0, 0))],
        out_specs=pl.BlockSpec((1, tq, LANES), lambda b_, h, i: (b_, i, h)),
        out_shape=jax.ShapeDtypeStruct((b, s, H_A * DV_A), BF16),
        scratch_shapes=[pltpu.VMEM((2 * tq, LANES), F32), pltpu.VMEM((2 * tq, 2 * DV_A), F32),
                        pltpu.VMEM((2, tq, tk), F32)],
        compiler_params=_params("parallel", "parallel", "arbitrary"),
        name="diff_attn",
    )(rel_bias.reshape(-1), q, k, v, lam_qk, subln_g.reshape(1, DV_A))


def _attn_cached_kernel(rb_ref, q_ref, kp_ref, vp_ref, kn_ref, vn_ref, lam_ref, g_ref, o_ref,
                        *, sq, past, lam_init):
    h = pl.program_id(1)
    q2 = _stack_maps(q_ref[0])
    near = min(past, LANES)
    far = jnp.full((sq, past - near), rb_ref[FAR_BUCKET * H_A + h], F32)
    bias_p = jnp.concatenate([far, _bias_tile(rb_ref, h, sq, near, -near, near)], axis=1) * LOG2E
    bias_n = _bias_tile(rb_ref, h, sq, sq, 0, sq) * LOG2E
    sp = lax.dot_general(q2, kp_ref[0], _NT, preferred_element_type=F32)
    sn = lax.dot_general(q2, kn_ref[0], _NT, preferred_element_type=F32)
    sp = (sp.reshape(2, sq, past) + bias_p).reshape(2 * sq, past)
    sn = (sn.reshape(2, sq, sq) + bias_n).reshape(2 * sq, sq)
    m = jnp.maximum(jnp.max(sp, axis=1, keepdims=True), jnp.max(sn, axis=1, keepdims=True))
    pp = jnp.exp2(sp - m)
    pn = jnp.exp2(sn - m)
    l = jnp.sum(pp, axis=1, keepdims=True) + jnp.sum(pn, axis=1, keepdims=True)
    acc = (jnp.dot(pp.astype(BF16), vp_ref[0], preferred_element_type=F32)
           + jnp.dot(pn.astype(BF16), vn_ref[0], preferred_element_type=F32))
    o_ref[0] = _diff_head_out(acc / l, sq, lam_ref, g_ref, lam_init).astype(o_ref.dtype)


def _attn_cached_call(q, k_past, v_past, k_new, v_new, rel_bias, lam_qk, subln_g, lam_init):
    b, sq, _ = q.shape
    past = k_past.shape[1]
    assert past % CHUNK == 0 and past >= LANES and sq % 8 == 0
    kern = functools.partial(_attn_cached_kernel, sq=sq, past=past, lam_init=lam_init)
    head = lambda rows: pl.BlockSpec((1, rows, LANES), lambda b_, h: (b_, 0, h))
    return pl.pallas_call(
        kern,
        grid=(b, H_A),
        in_specs=[pl.BlockSpec(memory_space=pltpu.SMEM), head(sq), head(past), head(past), head(sq), head(sq),
                  pl.BlockSpec((4, DH_A), lambda b_, h: (0, 0)),
                  pl.BlockSpec((1, DV_A), lambda b_, h: (0, 0))],
        out_specs=head(sq),
        out_shape=jax.ShapeDtypeStruct((b, sq, H_A * DV_A), BF16),
        compiler_params=_params("parallel", "parallel"),
        name="diff_attn_cached",
    )(rel_bias.reshape(-1), q, k_past, v_past, k_new, v_new, lam_qk, subln_g.reshape(1, DV_A))


def _split(x):
    hi = x.astype(BF16)
    return hi, (x - hi.astype(F32)).astype(BF16)


def _cat(pairs, axis):
    return tuple(jnp.concatenate(parts, axis=axis) for parts in zip(*pairs))


def _pdot(a, b, dims=_NN):
    (ah, al), (bh, bl) = a, b
    ca, cb = dims[0][0][0], dims[0][1][0]
    two = lax.dot_general(jnp.concatenate([ah, al], axis=ca), jnp.concatenate([bh, bh], axis=cb), dims,
                          preferred_element_type=F32)
    return two + lax.dot_general(ah, bl, dims, preferred_element_type=F32)


def _pdot_ones_rhs(a, ones):
    return jnp.dot(jnp.concatenate(a, axis=1), jnp.concatenate([ones, ones], axis=0),
                   preferred_element_type=F32)


def _pdot_ones_lhs(ones, b):
    return jnp.dot(jnp.concatenate([ones, ones], axis=1), jnp.concatenate(b, axis=0),
                   preferred_element_type=F32)


def _rwkv_kernel(prw_ref, shift_ref, s0_ref, mu_ref, wwa_hi_ref, wwa_lo_ref, g2_hi_ref, g2_lo_ref, vec_ref,
                 bd_ref, y_ref, sout_ref, carry_scr, state_scr, *, c_len):
    ci = pl.program_id(1)
    nc = pl.num_programs(1)
    n_pairs = H_B // 2
    n_seq = prw_ref.shape[0]

    @pl.when(ci == 0)
    def _():
        carry_scr[...] = shift_ref[...]
        state_scr[...] = s0_ref[...].reshape(state_scr.shape)

    row = lax.broadcasted_iota(jnp.int32, (c_len, N_RWKV), 0)
    xm = []
    for sq_i in range(n_seq):
        p = prw_ref[sq_i]
        prev = jnp.where(row == 0, carry_scr[sq_i], pltpu.roll(p, 1, axis=0))
        carry_scr[sq_i] = p[c_len - 1:c_len]
        xm.append(p + (prev - p) * mu_ref[...])
    xm = jnp.concatenate(xm, axis=0)

    w0, a0, k_k, k_a, r_k, lnx_g, lnx_b = (vec_ref[j:j + 1] for j in range(7))
    r = xm[:, :W_B]
    k = xm[:, W_B:2 * W_B]
    v = xm[:, 2 * W_B:3 * W_B]
    lora_in = xm[:, 3 * W_B:3 * W_B + W_LORA + A_LORA]
    lane = lax.broadcasted_iota(jnp.int32, lora_in.shape, 1)
    lora = _pdot(_split(jnp.where(lane < W_LORA, jnp.tanh(lora_in), lora_in)),
                 (wwa_hi_ref[...], wwa_lo_ref[...]))
    ww = -(w0 + lora[:, :W_B])
    w_log = -(jnp.maximum(ww, 0.0) + jnp.log(1.0 + jnp.exp(-jnp.abs(ww)))) - 0.5
    lw = -jnp.exp(w_log)
    a = _sigmoid(a0 + lora[:, W_B:])
    g = _pdot(_split(_sigmoid(xm[:, 3 * W_B + W_LORA + A_LORA:])), (g2_hi_ref[...], g2_lo_ref[...]))

    bd = bd_ref[...]

    def head_sum(x):
        return jnp.concatenate([_pdot_ones_rhs(_split(x[:, j * LANES:(j + 1) * LANES]), bd)
                                for j in range(W_B // LANES)], axis=1)

    kk = k * k_k
    kk = kk * lax.rsqrt(jnp.maximum(head_sum(kk * kk), 1e-24))
    km = k * (1.0 + (a - 1.0) * k_a)
    kb = kk * a
    bonus = head_sum(r * km * r_k) * v

    c2 = 2 * c_len
    tri_r = lax.broadcasted_iota(jnp.int32, (c_len, c_len), 0)
    tri_c = lax.broadcasted_iota(jnp.int32, (c_len, c_len), 1)
    tril = jnp.where(tri_r >= tri_c, 1.0, 0.0).astype(BF16)
    rr = lax.broadcasted_iota(jnp.int32, (c2, c2), 0)
    cc = lax.broadcasted_iota(jnp.int32, (c2, c2), 1)
    strict = rr > cc
    incl = rr >= cc
    eye = rr == cc
    same_blk = _div_pow2(rr, SOLVE_BLOCK) == _div_pow2(cc, SOLVE_BLOCK)
    head0 = lax.broadcasted_iota(jnp.int32, (c_len, LANES), 1) < DH_B

    def stack(x):
        z = jnp.zeros_like(x)
        return jnp.concatenate([jnp.where(head0, x, z), jnp.where(head0, z, x)], axis=0)

    def plus_eye(x):
        return jnp.where(eye, x + 1.0, x)

    def par(fn, *per_pair):
        return [fn(*args) for args in zip(*per_pair)]

    pairs = [(slice(i * c_len, (i + 1) * c_len), slice(j * LANES, (j + 1) * LANES))
             for i in range(n_seq) for j in range(n_pairs)]
    zero = jnp.zeros((c2, c2), F32)
    lw_p = [lw[sl] for sl in pairs]
    cum = par(lambda x: _pdot_ones_lhs(tril, _split(x)), lw_p)
    g_in = par(jnp.exp, cum)
    g_inv = par(lambda x: jnp.exp(-x), cum)
    a_hat = par(lambda sl, x, y: _split(stack(-kk[sl] * jnp.exp(x - y))), pairs, cum, lw_p)
    r_hat = par(lambda sl, x: _split(stack(r[sl] * x)), pairs, g_in)
    b_til = par(lambda sl, x: _split(stack(kb[sl] * x)), pairs, g_inv)
    k_til = par(lambda sl, x: _split(stack(km[sl] * x)), pairs, g_inv)
    v2 = par(lambda sl: _split(stack(v[sl])), pairs)
    lhs = par(lambda x, y: _cat([x, y], 0), a_hat, r_hat)
    if c2 % LANES == 0:
        both = par(lambda x, y, z: _pdot(x, _cat([y, z], 0), _NT), lhs, b_til, k_til)
        ab_rb = [x[:, :c2] for x in both]
        ak_rk = [x[:, c2:] for x in both]
    else:
        ab_rb = par(lambda x, y: _pdot(x, y, _NT), lhs, b_til)
        ak_rk = par(lambda x, y: _pdot(x, y, _NT), lhs, k_til)
    a_ab = par(lambda x: jnp.where(strict, x[:c2], zero), ab_rb)
    a_rb = par(lambda x: _split(jnp.where(incl, x[c2:], zero)), ab_rb)
    a_ak = par(lambda x: _split(jnp.where(strict, x[:c2], zero)), ak_rk)
    a_rk = par(lambda x: _split(jnp.where(incl, x[c2:], zero)), ak_rk)

    d1 = par(lambda x: jnp.where(same_blk, x, zero), a_ab)
    dinv = par(plus_eye, d1)
    dpow = par(_split, d1)
    for _ in range(int(math.log2(SOLVE_BLOCK)) - 1):
        dpow = par(lambda x: _split(_pdot(x, x)), dpow)
        dinv = par(lambda x, y: x + _pdot(_split(x), y), dinv, dpow)
    npow = par(lambda x, y, z: _split(_pdot(_split(x), _split(y - z))), dinv, a_ab, d1)
    t_mat = dinv
    n_levels = max(1, int(math.ceil(math.log2(c_len // SOLVE_BLOCK))))
    for level in range(n_levels):
        t_mat = par(lambda x, y: y + _pdot(x, _split(y)), npow, t_mat)
        if level + 1 < n_levels:
            npow = par(lambda x: _split(_pdot(x, x)), npow)
    t_mat = par(_split, t_mat)

    s_pair = [state_scr[j] for j in range(len(pairs))]
    s_split = par(_split, s_pair)
    rhs = par(lambda x, y, z, w: _split(_pdot(x, y, _NT) + _pdot(z, w)), a_hat, s_split, a_ak, v2)
    u = par(lambda x, y: _split(_pdot(x, y)), t_mat, rhs)
    o2 = par(lambda x, y, z, w, p_, q_: _pdot(x, y, _NT) + _pdot(z, w) + _pdot(p_, q_),
             r_hat, s_split, a_rk, v2, a_rb, u)
    outs = par(lambda x: x[:c_len] + x[c_len:], o2)
    s_add = par(lambda x, y, z, w: _pdot(_cat([x, y], 0), _cat([z, w], 0), _TN), v2, u, k_til, b_til)
    for j in range(len(pairs)):
        state_scr[j] = (s_pair[j] + s_add[j]) * g_in[j][c_len - 1:c_len]

    o = jnp.concatenate([jnp.concatenate(outs[i * n_pairs:(i + 1) * n_pairs], axis=1)
                         for i in range(n_seq)], axis=0)
    inv_dh = 1.0 / DH_B
    mean = head_sum(o) * inv_dh
    oc = o - mean
    var = head_sum(oc * oc) * inv_dh
    on = oc * lax.rsqrt(var + GN_EPS) * lnx_g + lnx_b
    y_ref[...] = ((on + bonus) * g).reshape(y_ref.shape).astype(y_ref.dtype)

    @pl.when(ci == nc - 1)
    def _():
        sout_ref[...] = state_scr[...].reshape(sout_ref.shape)


def _pair_states(wkv):
    b = wkv.shape[0]
    w = wkv.reshape(b, H_B // 2, 2, DH_B, DH_B)
    z = jnp.zeros_like(w[:, :, 0])
    top = jnp.concatenate([w[:, :, 0], z], axis=-1)
    bot = jnp.concatenate([z, w[:, :, 1]], axis=-1)
    return jnp.concatenate([top, bot], axis=-2)


def _unpair_states(sp):
    b = sp.shape[0]
    return jnp.stack([sp[:, :, :DH_B, :DH_B], sp[:, :, DH_B:, DH_B:]], axis=2).reshape(b, H_B, DH_B, DH_B)


def _rwkv_call(prw, shift, wkv, mu, wwa, g2, vecs):
    nseq, s, _ = prw.shape
    c_len = min(CHUNK, s)
    n_seq = 2 if nseq % 2 == 0 else 1
    assert s % c_len == 0 and c_len % SOLVE_BLOCK == 0
    const2 = lambda b_, c: (0, 0)
    head_of = jnp.arange(LANES, dtype=jnp.int32) // DH_B
    bd = (head_of[:, None] == head_of[None, :]).astype(BF16)
    wwa_hi, wwa_lo = _split(wwa)
    g2_hi, g2_lo = _split(g2)
    y, s_out = pl.pallas_call(
        functools.partial(_rwkv_kernel, c_len=c_len),
        grid=(nseq // n_seq, s // c_len),
        in_specs=[pl.BlockSpec((n_seq, c_len, N_RWKV), lambda b_, c: (b_, c, 0)),
                  pl.BlockSpec((n_seq, 1, N_RWKV), lambda b_, c: (b_, 0, 0)),
                  pl.BlockSpec((n_seq, H_B // 2, LANES, LANES), lambda b_, c: (b_, 0, 0, 0)),
                  pl.BlockSpec((1, N_RWKV), const2),
                  pl.BlockSpec(wwa.shape, const2),
                  pl.BlockSpec(wwa.shape, const2),
                  pl.BlockSpec(g2.shape, const2),
                  pl.BlockSpec(g2.shape, const2),
                  pl.BlockSpec(vecs.shape, const2),
                  pl.BlockSpec(bd.shape, const2)],
        out_specs=[pl.BlockSpec((n_seq, c_len, W_B), lambda b_, c: (b_, c, 0)),
                   pl.BlockSpec((n_seq, H_B // 2, LANES, LANES), lambda b_, c: (b_, 0, 0, 0))],
        out_shape=[jax.ShapeDtypeStruct((nseq, s, W_B), BF16),
                   jax.ShapeDtypeStruct((nseq, H_B // 2, LANES, LANES), F32)],
        scratch_shapes=[pltpu.VMEM((n_seq, 1, N_RWKV), F32),
                        pltpu.VMEM((n_seq * (H_B // 2), LANES, LANES), F32)],
        compiler_params=_params("parallel", "arbitrary"),
        name="rwkv7_chunk",
    )(prw, shift.reshape(nseq, 1, N_RWKV), _pair_states(wkv), mu.reshape(1, N_RWKV), wwa_hi, wwa_lo, g2_hi, g2_lo,
      vecs, bd)
    return y, _unpair_states(s_out)


def _merge_kernel(x_ref, mod_ref, ya_ref, yb_ref, wg_ref, bg_ref, wa_ref, wb_ref, wo_ref, g_ref, b_ref,
                  o_ref, *, alpha):
    g_, ts, d = x_ref.shape
    m = g_ * ts
    x = x_ref[...]
    shift, scale, gate = _mod_rows(mod_ref, 1)
    u = (_norm(x) * (1.0 + scale) + shift).reshape(m, d).astype(BF16)
    ma = jnp.dot(ya_ref[...].reshape(m, -1), wa_ref[...], preferred_element_type=F32)
    mb = jnp.dot(yb_ref[...].reshape(m, -1), wb_ref[...], preferred_element_type=F32)
    ga = _sigmoid(jnp.dot(u, wg_ref[:, :d], preferred_element_type=F32) + bg_ref[:, :d])
    gb = _sigmoid(jnp.dot(u, wg_ref[:, d:], preferred_element_type=F32) + bg_ref[:, d:])
    merged = (ga * ma + gb * mb).astype(BF16)
    z = jnp.dot(merged, wo_ref[...], preferred_element_type=F32).reshape(g_, ts, d)
    o_ref[...] = _norm(alpha * x + gate * z) * g_ref[...] + b_ref[...]


def _merge_call(x, mod, ya, yb, wg, bg, wa, wb, wo, ln_g, ln_b, alpha):
    nseq, s, d = x.shape
    g_, ts = _seq_tiling(nseq, s)
    const = lambda i, j: (0, 0)
    blk = lambda w: pl.BlockSpec((g_, ts, w), lambda i, j: (i, j, 0))
    full = lambda a: pl.BlockSpec(a.shape, const, pipeline_mode=pl.Buffered(1))
    return pl.pallas_call(
        functools.partial(_merge_kernel, alpha=alpha),
        grid=(nseq // g_, s // ts),
        in_specs=[blk(d), pl.BlockSpec((g_, 9, d), lambda i, j: (i, 0, 0)), blk(ya.shape[-1]), blk(yb.shape[-1]),
                  full(wg), pl.BlockSpec((1, 2 * d), const), full(wa), full(wb), full(wo),
                  pl.BlockSpec((1, d), const), pl.BlockSpec((1, d), const)],
        out_specs=blk(d),
        out_shape=jax.ShapeDtypeStruct(x.shape, F32),
        compiler_params=_params("parallel", "parallel"),
        name="merge_out",
    )(x, mod, ya, yb, wg, bg.reshape(1, 2 * d), wa, wb, wo, ln_g.reshape(1, d), ln_b.reshape(1, d))


def _layer(x, mod, lw, rel_bias, past_k, past_v, wkv, shift, lam_init, alpha, layer, depth, stacks):
    nseq, s, _ = x.shape
    x = _ffn_call(x, mod, lw["w_gu0"], lw["w_down0"], lw["ln_g"][0], lw["ln_b"][0], 0, alpha)
    q, k_stack, v_stack, kb, vb, prw = _mixin_call(x, mod, lw["w_in"], layer, depth, stacks)
    if past_k is None:
        ya = _attn_call(q, kb, vb, rel_bias, lw["lam_qk"], lw["subln_g"], lam_init)
    else:
        past = past_k.shape[1]
        ya = _attn_cached_call(q, past_k.reshape(nseq, past, N_QK).astype(BF16),
                               past_v.reshape(nseq, past, N_QK).astype(BF16), kb, vb, rel_bias,
                               lw["lam_qk"], lw["subln_g"], lam_init)
    yb, wkv_new = _rwkv_call(prw, shift, wkv, lw["rw_mu"], lw["rw_wwa"], lw["rw_g2"], lw["rw_vecs"])
    x = _merge_call(x, mod, ya, yb, lw["w_gate"], lw["b_gate"], lw["w_br_a"], lw["w_br_b"], lw["w_o"],
                    lw["ln_g"][1], lw["ln_b"][1], alpha)
    x = _ffn_call(x, mod, lw["w_gu1"], lw["w_down1"], lw["ln_g"][2], lw["ln_b"][2], 2, alpha)
    return x, (k_stack, v_stack), wkv_new, prw[:, -1]


def kernel(x_prompt, x_sample, c_prompt, c_sample, cache_k, cache_v, state_wkv, state_shift, rel_bias, w_ada, b_ada, ln_g, ln_b, w_gu, w_down, w_in, lam_qk, subln_g, rw_mu, rw_w0, rw_w2, rw_a0, rw_a2, rw_g2, rw_k_k, rw_k_a, rw_r_k, rw_lnx_g, rw_lnx_b, w_br_a, w_br_b, w_gate, b_gate, w_o):
    depth = w_ada.shape[0]
    b_p = x_prompt.shape[0]
    alpha = (2 * depth) ** 0.25
    mod = _ada_call(jnp.concatenate([c_prompt, c_sample], axis=0), w_ada, b_ada)
    mod = mod.reshape(depth, -1, 9, D_MODEL)

    w_gu_b, w_down_b, w_in_b = w_gu.astype(BF16), w_down.astype(BF16), w_in.astype(BF16)
    w_gate_b, w_br_a_b, w_br_b_b, w_o_b = (w.astype(BF16) for w in (w_gate, w_br_a, w_br_b, w_o))
    zeros_l = jnp.zeros((depth, W_LORA, W_B), F32)
    rw_wwa = jnp.concatenate([jnp.concatenate([rw_w2, zeros_l], axis=2),
                              jnp.concatenate([zeros_l, rw_a2], axis=2)], axis=1)
    rw_vecs = jnp.stack([rw_w0, rw_a0, rw_k_k, rw_k_a, rw_r_k.reshape(depth, W_B), rw_lnx_g, rw_lnx_b,
                         jnp.zeros_like(rw_w0)], axis=1)

    zero_wkv = jnp.zeros((b_p, H_B, DH_B, DH_B), F32)
    zero_shift = jnp.zeros((b_p, N_RWKV), F32)
    hp, hd = x_prompt, x_sample
    kv_p = kv_d = None
    wkv_p, wkv_d, shift_p, shift_d = [], [], [], []
    for l in range(depth):
        lw = {"w_gu0": w_gu_b[l, 0], "w_gu1": w_gu_b[l, 1], "w_down0": w_down_b[l, 0], "w_down1": w_down_b[l, 1],
              "w_in": w_in_b[l], "ln_g": ln_g[l], "ln_b": ln_b[l], "lam_qk": lam_qk[l], "subln_g": subln_g[l],
              "rw_mu": rw_mu[l], "rw_wwa": rw_wwa[l], "rw_g2": rw_g2[l], "rw_vecs": rw_vecs[l],
              "w_gate": w_gate_b[l], "b_gate": b_gate[l], "w_br_a": w_br_a_b[l], "w_br_b": w_br_b_b[l],
              "w_o": w_o_b[l]}
        lam_init = 0.8 - 0.6 * math.exp(-0.3 * l)
        hp, kv_p, wkv, shift = _layer(hp, mod[l, :b_p], lw, rel_bias, None, None, zero_wkv, zero_shift,
                                      lam_init, alpha, l, depth, kv_p)
        wkv_p.append(wkv)
        shift_p.append(shift)
        hd, kv_d, wkv, shift = _layer(hd, mod[l, b_p:], lw, rel_bias, cache_k[l], cache_v[l], state_wkv[l],
                                      state_shift[l], lam_init, alpha, l, depth, kv_d)
        wkv_d.append(wkv)
        shift_d.append(shift)
    return (hp, hd, kv_p[0], kv_p[1], jnp.stack(wkv_p), jnp.stack(shift_p),
            kv_d[0], kv_d[1], jnp.stack(wkv_d), jnp.stack(shift_d))
```

```python
import functools
import math

import jax
import jax.numpy as jnp
from jax import lax
from jax.experimental import pallas as pl
from jax.experimental.pallas import tpu as pltpu

F32 = jnp.float32
BF16 = jnp.bfloat16

D_MODEL = 1024
CHUNK = 64
H_A = 4
DH_A = 64
DV_A = 128
N_QK = 512
N_ATT = 1536
H_B = 8
DH_B = 64
W_B = 512
W_LORA = 64
A_LORA = 64
G_LORA = 128
N_RWKV = 1792
D_FF = 2816
N_BUCKETS = 32
LN_EPS = 1e-5
GN_EPS = 64e-5

LANES = 128
VMEM_LIMIT = 56 * 1024 * 1024
NEG_BIG = -1e30
FAR_BUCKET = N_BUCKETS // 2 - 1
BUCKET_STARTS = (12, 16, 23, 32, 46, 64, 91)

ATTN_TILE = 512
ROW_GROUPS = 4
LOG2E = 1.4426950408889634
_NN = (((1,), (0,)), ((), ()))
_NT = (((1,), (1,)), ((), ()))
_TN = (((0,), (0,)), ((), ()))
SOLVE_BLOCK = 16


def _params(*sem):
    return pltpu.CompilerParams(dimension_semantics=sem, vmem_limit_bytes=VMEM_LIMIT)


def _norm(x):
    mu = jnp.mean(x, axis=-1, keepdims=True)
    xc = x - mu
    var = jnp.mean(xc * xc, axis=-1, keepdims=True)
    return xc * lax.rsqrt(var + LN_EPS)


def _sigmoid(x):
    return 1.0 / (1.0 + jnp.exp(-x))


def _div_pow2(x, n):
    assert n & (n - 1) == 0
    return lax.shift_right_logical(x, n.bit_length() - 1)


def _mod_rows(mod_ref, sub):
    return tuple(mod_ref[:, 3 * sub + j:3 * sub + j + 1, :] for j in range(3))


def _ada_kernel(c_ref, w_ref, b_ref, o_ref):
    c = c_ref[...]
    sc = (c * _sigmoid(c)).astype(BF16)
    o_ref[0] = jnp.dot(sc, w_ref[0].astype(BF16), preferred_element_type=F32) + b_ref[0]


def _ada_call(c, w_ada, b_ada):
    depth, d, n = w_ada.shape
    nseq = c.shape[0]
    tn = 1152
    return pl.pallas_call(
        _ada_kernel,
        grid=(depth, n // tn),
        in_specs=[pl.BlockSpec((nseq, d), lambda l, j: (0, 0)),
                  pl.BlockSpec((1, d, tn), lambda l, j: (l, 0, j)),
                  pl.BlockSpec((1, 1, tn), lambda l, j: (l, 0, j))],
        out_specs=pl.BlockSpec((1, nseq, tn), lambda l, j: (l, 0, j)),
        out_shape=jax.ShapeDtypeStruct((depth, nseq, n), F32),
        compiler_params=_params("parallel", "parallel"),
        name="ada_mod",
    )(c, w_ada, b_ada.reshape(depth, 1, n))


def _ffn_kernel(x_ref, mod_ref, wgu_ref, wd_ref, g_ref, b_ref, o_ref, *, sub, fc, alpha):
    g_, ts, d = x_ref.shape
    x = x_ref[...]
    shift, scale, gate = _mod_rows(mod_ref, sub)
    u = (_norm(x) * (1.0 + scale) + shift).reshape(g_ * ts, d).astype(BF16)
    f = jnp.zeros((g_ * ts, d), F32)
    for c in range(D_FF // fc):
        hg = jnp.dot(u, wgu_ref[:, c * fc:(c + 1) * fc], preferred_element_type=F32)
        hu = jnp.dot(u, wgu_ref[:, D_FF + c * fc:D_FF + (c + 1) * fc], preferred_element_type=F32)
        act = (hg * _sigmoid(hg) * hu).astype(BF16)
        f = f + jnp.dot(act, wd_ref[c * fc:(c + 1) * fc, :], preferred_element_type=F32)
    y = alpha * x + (0.5 * gate) * f.reshape(g_, ts, d)
    o_ref[...] = _norm(y) * g_ref[...] + b_ref[...]


def _seq_tiling(nseq, s):
    if s >= 512:
        return 1, 512
    g = max(1, min(nseq, 256 // s))
    while nseq % g:
        g -= 1
    return g, s


def _ffn_call(x, mod, wgu, wd, ln_g, ln_b, sub, alpha):
    nseq, s, d = x.shape
    g_, ts = _seq_tiling(nseq, s)
    const = lambda i, j: (0, 0)
    return pl.pallas_call(
        functools.partial(_ffn_kernel, sub=sub, fc=D_FF // 2, alpha=alpha),
        grid=(nseq // g_, s // ts),
        in_specs=[pl.BlockSpec((g_, ts, d), lambda i, j: (i, j, 0)),
                  pl.BlockSpec((g_, 9, d), lambda i, j: (i, 0, 0)),
                  pl.BlockSpec(wgu.shape, const, pipeline_mode=pl.Buffered(1)),
                  pl.BlockSpec(wd.shape, const, pipeline_mode=pl.Buffered(1)),
                  pl.BlockSpec((1, d), const),
                  pl.BlockSpec((1, d), const)],
        out_specs=pl.BlockSpec((g_, ts, d), lambda i, j: (i, j, 0)),
        out_shape=jax.ShapeDtypeStruct(x.shape, F32),
        compiler_params=_params("parallel", "parallel"),
        name="ffn_block",
    )(x, mod, wgu, wd, ln_g.reshape(1, d), ln_b.reshape(1, d))


def _mixin_kernel(x_ref, mod_ref, win_ref, *refs):
    q_ref, k_ref, v_ref, kb_ref, vb_ref, prw_ref = refs[-6:]
    g_, ts, d = x_ref.shape
    shift, scale, _ = _mod_rows(mod_ref, 1)
    u = (_norm(x_ref[...]) * (1.0 + scale) + shift).reshape(g_ * ts, d).astype(BF16)

    def proj(lo, hi):
        return jnp.dot(u, win_ref[:, lo:hi], preferred_element_type=F32).reshape(g_, ts, hi - lo)

    q = proj(0, N_QK)
    q_ref[...] = (q * (DH_A ** -0.5 * LOG2E)).astype(BF16)
    k = proj(N_QK, 2 * N_QK)
    kb_ref[...] = k.astype(BF16)
    v = proj(2 * N_QK, N_ATT)
    vb_ref[...] = v.astype(BF16)
    for h in range(H_A):
        k_ref[0, :, :, h, :] = k[:, :, h * LANES:(h + 1) * LANES]
        v_ref[0, :, :, h, :] = v[:, :, h * LANES:(h + 1) * LANES]
    prw_ref[...] = proj(N_ATT, N_ATT + N_RWKV)


def _mixin_call(x, mod, win, layer, depth, stacks):
    nseq, s, d = x.shape
    g_, ts = _seq_tiling(nseq, s)
    const = lambda i, j: (0, 0)
    blk = lambda w: pl.BlockSpec((g_, ts, w), lambda i, j: (i, j, 0))
    sds = lambda w, dt: jax.ShapeDtypeStruct((nseq, s, w), dt)
    slab = pl.BlockSpec((1, g_, ts, H_A, LANES), lambda i, j: (layer, i, j, 0, 0))
    stack_sds = jax.ShapeDtypeStruct((depth, nseq, s, H_A, LANES), F32)
    in_specs = [blk(d),
                pl.BlockSpec((g_, 9, d), lambda i, j: (i, 0, 0)),
                pl.BlockSpec(win.shape, const, pipeline_mode=pl.Buffered(1))]
    args, aliases = [x, mod, win], {}
    if stacks is not None:
        in_specs += [pl.BlockSpec(memory_space=pl.ANY)] * 2
        args += list(stacks)
        aliases = {3: 1, 4: 2}
    return pl.pallas_call(
        _mixin_kernel,
        grid=(nseq // g_, s // ts),
        in_specs=in_specs,
        out_specs=[blk(N_QK), slab, slab, blk(N_QK), blk(N_QK), blk(N_RWKV)],
        out_shape=[sds(N_QK, BF16), stack_sds, stack_sds, sds(N_QK, BF16), sds(N_QK, BF16), sds(N_RWKV, F32)],
        input_output_aliases=aliases,
        compiler_params=_params("parallel", "parallel"),
        name="mixer_in",
    )(*args)


def _bias_tile(rb_ref, h, tq, tk, offset, valid_cols):
    r = lax.broadcasted_iota(jnp.int32, (tq, tk), 0)
    c = lax.broadcasted_iota(jnp.int32, (tq, tk), 1)
    rel = c - r + offset
    n = jnp.abs(rel)
    log_bucket = jnp.full((tq, tk), N_BUCKETS // 4, jnp.int32)
    for start in BUCKET_STARTS:
        log_bucket = log_bucket + jnp.where(n >= start, 1, 0)
    bucket = jnp.where(rel > 0, N_BUCKETS // 2, 0) + jnp.where(n < N_BUCKETS // 4, n, log_bucket)
    bias = jnp.zeros((tq, tk), F32)
    for b in range(N_BUCKETS):
        bias = jnp.where(bucket == b, rb_ref[b * H_A + h], bias)
    if offset == 0:
        visible = jnp.logical_and(_div_pow2(c, CHUNK) <= _div_pow2(r, CHUNK), c < valid_cols)
        bias = jnp.where(visible, bias, NEG_BIG)
    return bias


def _stack_maps(q):
    lane = lax.broadcasted_iota(jnp.int32, q.shape, 1)
    zero = jnp.zeros_like(q)
    return jnp.concatenate([jnp.where(lane < DH_A, q, zero), jnp.where(lane < DH_A, zero, q)], axis=0)


def _diff_head_out(o_all, tq, lam_ref, g_ref, lam_init):
    lq = lam_ref[...]
    lam = (jnp.exp(jnp.sum(lq[0:1] * lq[1:2], axis=1, keepdims=True))
           - jnp.exp(jnp.sum(lq[2:3] * lq[3:4], axis=1, keepdims=True)) + lam_init)
    o = o_all[:tq] - lam * o_all[tq:]
    o = o * lax.rsqrt(jnp.mean(o * o, axis=-1, keepdims=True) + LN_EPS)
    return o * g_ref[...] * (1.0 - lam_init)


def _attn_kernel(rb_ref, q_ref, k_ref, v_ref, lam_ref, g_ref, o_ref, m_scr, acc_scr, bias_scr,
                 *, tq, tk, lam_init):
    h = pl.program_id(1)
    i = pl.program_id(2)

    @pl.when(i == 0)
    def _():
        for slot, offset in ((0, -tk), (1, 0)):
            bias_scr[slot] = _bias_tile(rb_ref, h, tq, tk, offset, tk) * LOG2E

    q2 = _stack_maps(q_ref[0])
    m_scr[...] = jnp.full(m_scr.shape, NEG_BIG, F32)
    acc_scr[...] = jnp.zeros(acc_scr.shape, F32)
    ones = jnp.ones((tk, LANES), BF16)

    def tiles(js, bias_slots, shift):
        rows = 2 * tq // ROW_GROUPS
        starts = [pl.multiple_of(j * tk, tk) for j in js]

        def score(start, g):
            return lax.dot_general(q2[g * rows:(g + 1) * rows], k_ref[0, pl.ds(start, tk), :], _NT,
                                   preferred_element_type=F32)

        def update(start_v1, slot, g, s):
            sl = slice(g * rows, (g + 1) * rows)
            if slot is not None:
                r0 = (g * rows) % tq
                s = s + bias_scr[slot, r0:r0 + rows, :]
            m_prev = m_scr[sl]
            m_new = jnp.maximum(m_prev, jnp.max(s, axis=1, keepdims=True) + shift)
            p = jnp.exp2(s - jnp.tile(m_new - shift, (1, tk // LANES)))
            alpha = jnp.exp2(m_prev - m_new)
            acc_scr[sl] = (jnp.tile(alpha, (1, 2)) * acc_scr[sl]
                           + jnp.dot(p.astype(BF16), start_v1, preferred_element_type=F32))
            m_scr[sl] = m_new

        scores = [score(starts[0], g) for g in range(ROW_GROUPS)]
        for t, (start, slot) in enumerate(zip(starts, bias_slots)):
            v1 = jnp.concatenate([v_ref[0, pl.ds(start, tk), :], ones], axis=1)
            nxt = []
            for g in range(ROW_GROUPS):
                if t + 1 < len(starts):
                    nxt.append(score(starts[t + 1], g))
                update(v1, slot, g, scores[g])
            scores = nxt

    far_shift = rb_ref[FAR_BUCKET * H_A + h] * LOG2E
    n_far = jnp.maximum(i - 1, 0)

    def far_body(jj, carry):
        tiles([2 * jj, 2 * jj + 1], [None, None], far_shift)
        return carry

    lax.fori_loop(0, n_far // 2, far_body, 0)

    @pl.when(n_far % 2 == 1)
    def _():
        tiles([n_far - 1], [None], far_shift)

    @pl.when(i >= 1)
    def _():
        tiles([i - 1, i], [0, 1], 0.0)

    @pl.when(i == 0)
    def _():
        tiles([i], [1], 0.0)

    acc = acc_scr[...]
    o_ref[0] = _diff_head_out(acc[:, :DV_A] / acc[:, DV_A:], tq, lam_ref, g_ref, lam_init).astype(o_ref.dtype)


def _attn_call(q, k, v, rel_bias, lam_qk, subln_g, lam_init):
    b, s, _ = q.shape
    tq = tk = ATTN_TILE
    assert s % tk == 0 and tk % CHUNK == 0 and (2 * tq) % ROW_GROUPS == 0 and tq % (2 * tq // ROW_GROUPS) == 0
    kern = functools.partial(_attn_kernel, tq=tq, tk=tk, lam_init=lam_init)
    return pl.pallas_call(
        kern,
        grid=(b, H_A, s // tq),
        in_specs=[pl.BlockSpec(memory_space=pltpu.SMEM),
                  pl.BlockSpec((1, tq, LANES), lambda b_, h, i: (b_, i, h)),
                  pl.BlockSpec((1, s, LANES), lambda b_, h, i: (b_, 0, h)),
                  pl.BlockSpec((1, s, LANES), lambda b_, h, i: (b_, 0, h)),
                  pl.BlockSpec((4, DH_A), lambda b_, h, i: (0, 0)),
                  pl.BlockSpec((1, DV_A), lambda b_, h, i: (0, 0))],
        out_specs=pl.BlockSpec((1, tq, LANES), lambda b_, h, i: (b_, i, h)),
        out_shape=jax.ShapeDtypeStruct((b, s, H_A * DV_A), BF16),
        scratch_shapes=[pltpu.VMEM((2 * tq, LANES), F32), pltpu.VMEM((2 * tq, 2 * DV_A), F32),
                        pltpu.VMEM((2, tq, tk), F32)],
        compiler_params=_params("parallel", "parallel", "arbitrary"),
        name="diff_attn",
    )(rel_bias.reshape(-1), q, k, v, lam_qk, subln_g.reshape(1, DV_A))


def _attn_cached_kernel(rb_ref, q_ref, kc_ref, vc_ref, kn_ref, vn_ref, lam_ref, g_ref, o_ref,
                        *, sq, past, lam_init):
    near = min(past, LANES)
    for h in range(H_A):
        lanes = slice(h * LANES, (h + 1) * LANES)
        q2 = _stack_maps(q_ref[0, :, lanes])
        far = jnp.full((sq, past - near), rb_ref[FAR_BUCKET * H_A + h], F32)
        bias_p = jnp.concatenate([far, _bias_tile(rb_ref, h, sq, near, -near, near)], axis=1) * LOG2E
        bias_n = _bias_tile(rb_ref, h, sq, sq, 0, sq) * LOG2E
        kp = kc_ref[0, 0, pl.ds(h, past, stride=H_A), :].astype(BF16)
        vp = vc_ref[0, 0, pl.ds(h, past, stride=H_A), :].astype(BF16)
        sp = lax.dot_general(q2, kp, _NT, preferred_element_type=F32)
        sn = lax.dot_general(q2, kn_ref[0, :, lanes], _NT, preferred_element_type=F32)
        sp = (sp.reshape(2, sq, past) + bias_p).reshape(2 * sq, past)
        sn = (sn.reshape(2, sq, sq) + bias_n).reshape(2 * sq, sq)
        m = jnp.maximum(jnp.max(sp, axis=1, keepdims=True), jnp.max(sn, axis=1, keepdims=True))
        pp = jnp.exp2(sp - m)
        pn = jnp.exp2(sn - m)
        l = jnp.sum(pp, axis=1, keepdims=True) + jnp.sum(pn, axis=1, keepdims=True)
        acc = (jnp.dot(pp.astype(BF16), vp, preferred_element_type=F32)
               + jnp.dot(pn.astype(BF16), vn_ref[0, :, lanes], preferred_element_type=F32))
        o_ref[0, :, lanes] = _diff_head_out(acc / l, sq, lam_ref, g_ref, lam_init).astype(o_ref.dtype)


def _attn_cached_call(q, cache_k, cache_v, layer, k_new, v_new, rel_bias, lam_qk, subln_g, lam_init):
    b, sq, _ = q.shape
    depth, _, past = cache_k.shape[:3]
    assert past % CHUNK == 0 and past >= LANES and sq % 8 == 0
    kern = functools.partial(_attn_cached_kernel, sq=sq, past=past, lam_init=lam_init)
    new = pl.BlockSpec((1, sq, H_A * LANES), lambda b_: (b_, 0, 0))
    slab = pl.BlockSpec((1, 1, past * H_A, LANES), lambda b_: (layer, b_, 0, 0))
    rows = lambda c: c.reshape(depth, b, past * H_A, LANES)
    return pl.pallas_call(
        kern,
        grid=(b,),
        in_specs=[pl.BlockSpec(memory_space=pltpu.SMEM), new, slab, slab, new, new,
                  pl.BlockSpec((4, DH_A), lambda b_: (0, 0)),
                  pl.BlockSpec((1, DV_A), lambda b_: (0, 0))],
        out_specs=new,
        out_shape=jax.ShapeDtypeStruct((b, sq, H_A * DV_A), BF16),
        compiler_params=_params("parallel"),
        name="diff_attn_cached",
    )(rel_bias.reshape(-1), q, rows(cache_k), rows(cache_v), k_new, v_new, lam_qk, subln_g.reshape(1, DV_A))


def _split(x):
    hi = x.astype(BF16)
    return hi, (x - hi.astype(F32)).astype(BF16)


def _cat(pairs, axis):
    return tuple(jnp.concatenate(parts, axis=axis) for parts in zip(*pairs))


def _pdot(a, b, dims=_NN):
    (ah, al), (bh, bl) = a, b
    ca, cb = dims[0][0][0], dims[0][1][0]
    two = lax.dot_general(jnp.concatenate([ah, al], axis=ca), jnp.concatenate([bh, bh], axis=cb), dims,
                          preferred_element_type=F32)
    return two + lax.dot_general(ah, bl, dims, preferred_element_type=F32)


def _pdot_rhs16(a, b, dims=_NN):
    (ah, al), (bh, _) = a, b
    ca, cb = dims[0][0][0], dims[0][1][0]
    return lax.dot_general(jnp.concatenate([ah, al], axis=ca), jnp.concatenate([bh, bh], axis=cb), dims,
                           preferred_element_type=F32)


def _pdot_ones_rhs(a, ones):
    return jnp.dot(jnp.concatenate(a, axis=1), jnp.concatenate([ones, ones], axis=0),
                   preferred_element_type=F32)


def _pdot_ones_lhs(ones, b):
    return jnp.dot(jnp.concatenate([ones, ones], axis=1), jnp.concatenate(b, axis=0),
                   preferred_element_type=F32)


def _rwkv_kernel(prw_ref, shift_ref, s0_ref, mu_ref, wwa_hi_ref, wwa_lo_ref, g2_hi_ref, g2_lo_ref, vec_ref,
                 bd_ref, y_ref, sout_ref, carry_scr, state_scr, *, c_len):
    ci = pl.program_id(1)
    nc = pl.num_programs(1)
    n_pairs = H_B // 2
    n_seq = prw_ref.shape[0]

    @pl.when(ci == 0)
    def _():
        carry_scr[...] = shift_ref[...]
        state_scr[...] = s0_ref[...].reshape(state_scr.shape)

    row = lax.broadcasted_iota(jnp.int32, (c_len, N_RWKV), 0)
    xm = []
    for sq_i in range(n_seq):
        p = prw_ref[sq_i]
        prev = jnp.where(row == 0, carry_scr[sq_i], pltpu.roll(p, 1, axis=0))
        carry_scr[sq_i] = p[c_len - 1:c_len]
        xm.append(p + (prev - p) * mu_ref[...])
    xm = jnp.concatenate(xm, axis=0)

    w0, a0, k_k, k_a, r_k, lnx_g, lnx_b = (vec_ref[j:j + 1] for j in range(7))
    r = xm[:, :W_B]
    k = xm[:, W_B:2 * W_B]
    v = xm[:, 2 * W_B:3 * W_B]
    lora_in = xm[:, 3 * W_B:3 * W_B + W_LORA + A_LORA]
    lane = lax.broadcasted_iota(jnp.int32, lora_in.shape, 1)
    lora = _pdot(_split(jnp.where(lane < W_LORA, jnp.tanh(lora_in), lora_in)),
                 (wwa_hi_ref[...], wwa_lo_ref[...]))
    ww = -(w0 + lora[:, :W_B])
    w_log = -(jnp.maximum(ww, 0.0) + jnp.log(1.0 + jnp.exp(-jnp.abs(ww)))) - 0.5
    lw = -jnp.exp(w_log)
    a = _sigmoid(a0 + lora[:, W_B:])
    g = _pdot(_split(_sigmoid(xm[:, 3 * W_B + W_LORA + A_LORA:])), (g2_hi_ref[...], g2_lo_ref[...]))

    bd = bd_ref[...]

    def head_sum(x):
        return jnp.concatenate([_pdot_ones_rhs(_split(x[:, j * LANES:(j + 1) * LANES]), bd)
                                for j in range(W_B // LANES)], axis=1)

    kk = k * k_k
    kk = kk * lax.rsqrt(jnp.maximum(head_sum(kk * kk), 1e-24))
    km = k * (1.0 + (a - 1.0) * k_a)
    kb = kk * a
    bonus = head_sum(r * km * r_k) * v

    c2 = 2 * c_len
    tri_r = lax.broadcasted_iota(jnp.int32, (c_len, c_len), 0)
    tri_c = lax.broadcasted_iota(jnp.int32, (c_len, c_len), 1)
    tril = jnp.where(tri_r >= tri_c, 1.0, 0.0).astype(BF16)
    rr = lax.broadcasted_iota(jnp.int32, (c2, c2), 0)
    cc = lax.broadcasted_iota(jnp.int32, (c2, c2), 1)
    strict = rr > cc
    incl = rr >= cc
    eye = rr == cc
    same_blk = _div_pow2(rr, SOLVE_BLOCK) == _div_pow2(cc, SOLVE_BLOCK)
    head0 = lax.broadcasted_iota(jnp.int32, (c_len, LANES), 1) < DH_B

    def stack(x):
        z = jnp.zeros_like(x)
        return jnp.concatenate([jnp.where(head0, x, z), jnp.where(head0, z, x)], axis=0)

    def plus_eye(x):
        return jnp.where(eye, x + 1.0, x)

    def par(fn, *per_pair):
        return [fn(*args) for args in zip(*per_pair)]

    pairs = [(slice(i * c_len, (i + 1) * c_len), slice(j * LANES, (j + 1) * LANES))
             for i in range(n_seq) for j in range(n_pairs)]
    zero = jnp.zeros((c2, c2), F32)
    lw_p = [lw[sl] for sl in pairs]
    cum = par(lambda x: _pdot_ones_lhs(tril, _split(x)), lw_p)
    g_in = par(jnp.exp, cum)
    g_inv = par(lambda x: jnp.exp(-x), cum)
    a_hat = par(lambda sl, x, y: _split(stack(-kk[sl] * jnp.exp(x - y))), pairs, cum, lw_p)
    r_hat = par(lambda sl, x: _split(stack(r[sl] * x)), pairs, g_in)
    b_til = par(lambda sl, x: _split(stack(kb[sl] * x)), pairs, g_inv)
    k_til = par(lambda sl, x: _split(stack(km[sl] * x)), pairs, g_inv)
    v2 = par(lambda sl: _split(stack(v[sl])), pairs)
    lhs = par(lambda x, y: _cat([x, y], 0), a_hat, r_hat)
    if c2 % LANES == 0:
        both = par(lambda x, y, z: _pdot_rhs16(x, _cat([y, z], 0), _NT), lhs, b_til, k_til)
        ab_rb = [x[:, :c2] for x in both]
        ak_rk = [x[:, c2:] for x in both]
    else:
        ab_rb = par(lambda x, y: _pdot_rhs16(x, y, _NT), lhs, b_til)
        ak_rk = par(lambda x, y: _pdot_rhs16(x, y, _NT), lhs, k_til)
    a_ab = par(lambda x: jnp.where(strict, x[:c2], zero), ab_rb)
    a_rb = par(lambda x: _split(jnp.where(incl, x[c2:], zero)), ab_rb)
    a_ak = par(lambda x: _split(jnp.where(strict, x[:c2], zero)), ak_rk)
    a_rk = par(lambda x: _split(jnp.where(incl, x[c2:], zero)), ak_rk)

    d1 = par(lambda x: jnp.where(same_blk, x, zero), a_ab)
    dinv = par(plus_eye, d1)
    dpow = par(_split, d1)
    for _ in range(int(math.log2(SOLVE_BLOCK)) - 1):
        dpow = par(lambda x: _split(_pdot(x, x)), dpow)
        dinv = par(lambda x, y: x + _pdot(_split(x), y), dinv, dpow)
    npow = par(lambda x, y, z: _split(_pdot(_split(x), _split(y - z))), dinv, a_ab, d1)
    t_mat = dinv
    n_levels = max(1, int(math.ceil(math.log2(c_len // SOLVE_BLOCK))))
    for level in range(n_levels):
        t_mat = par(lambda x, y: y + _pdot(x, _split(y)), npow, t_mat)
        if level + 1 < n_levels:
            npow = par(lambda x: _split(_pdot(x, x)), npow)
    t_mat = par(_split, t_mat)

    s_pair = [state_scr[j] for j in range(len(pairs))]
    s_split = par(_split, s_pair)
    rhs = par(lambda x, y, z, w: _split(_pdot_rhs16(x, y, _NT) + _pdot_rhs16(z, w)), a_hat, s_split, a_ak, v2)
    u = par(lambda x, y: _split(_pdot(x, y)), t_mat, rhs)
    o2 = par(lambda x, y, z, w, p_, q_: _pdot_rhs16(x, y, _NT) + _pdot_rhs16(z, w) + _pdot_rhs16(p_, q_),
             r_hat, s_split, a_rk, v2, a_rb, u)
    outs = par(lambda x: x[:c_len] + x[c_len:], o2)
    s_add = par(lambda x, y, z, w: _pdot_rhs16(_cat([x, y], 0), _cat([z, w], 0), _TN), v2, u, k_til, b_til)
    for j in range(len(pairs)):
        state_scr[j] = (s_pair[j] + s_add[j]) * g_in[j][c_len - 1:c_len]

    o = jnp.concatenate([jnp.concatenate(outs[i * n_pairs:(i + 1) * n_pairs], axis=1)
                         for i in range(n_seq)], axis=0)
    inv_dh = 1.0 / DH_B
    mean = head_sum(o) * inv_dh
    oc = o - mean
    var = head_sum(oc * oc) * inv_dh
    on = oc * lax.rsqrt(var + GN_EPS) * lnx_g + lnx_b
    y_ref[...] = ((on + bonus) * g).reshape(y_ref.shape).astype(y_ref.dtype)

    @pl.when(ci == nc - 1)
    def _():
        sout_ref[...] = state_scr[...].reshape(sout_ref.shape)


def _pair_states(wkv):
    b = wkv.shape[0]
    w = wkv.reshape(b, H_B // 2, 2, DH_B, DH_B)
    z = jnp.zeros_like(w[:, :, 0])
    top = jnp.concatenate([w[:, :, 0], z], axis=-1)
    bot = jnp.concatenate([z, w[:, :, 1]], axis=-1)
    return jnp.concatenate([top, bot], axis=-2)


def _unpair_states(sp):
    b = sp.shape[0]
    return jnp.stack([sp[:, :, :DH_B, :DH_B], sp[:, :, DH_B:, DH_B:]], axis=2).reshape(b, H_B, DH_B, DH_B)


def _rwkv_call(prw, shift, wkv, mu, wwa, g2, vecs):
    nseq, s, _ = prw.shape
    c_len = min(CHUNK, s)
    n_seq = 2 if nseq % 2 == 0 else 1
    assert s % c_len == 0 and c_len % SOLVE_BLOCK == 0
    const2 = lambda b_, c: (0, 0)
    head_of = jnp.arange(LANES, dtype=jnp.int32) // DH_B
    bd = (head_of[:, None] == head_of[None, :]).astype(BF16)
    wwa_hi, wwa_lo = _split(wwa)
    g2_hi, g2_lo = _split(g2)
    y, s_out = pl.pallas_call(
        functools.partial(_rwkv_kernel, c_len=c_len),
        grid=(nseq // n_seq, s // c_len),
        in_specs=[pl.BlockSpec((n_seq, c_len, N_RWKV), lambda b_, c: (b_, c, 0)),
                  pl.BlockSpec((n_seq, 1, N_RWKV), lambda b_, c: (b_, 0, 0)),
                  pl.BlockSpec((n_seq, H_B // 2, LANES, LANES), lambda b_, c: (b_, 0, 0, 0)),
                  pl.BlockSpec((1, N_RWKV), const2),
                  pl.BlockSpec(wwa.shape, const2),
                  pl.BlockSpec(wwa.shape, const2),
                  pl.BlockSpec(g2.shape, const2),
                  pl.BlockSpec(g2.shape, const2),
                  pl.BlockSpec(vecs.shape, const2),
                  pl.BlockSpec(bd.shape, const2)],
        out_specs=[pl.BlockSpec((n_seq, c_len, W_B), lambda b_, c: (b_, c, 0)),
                   pl.BlockSpec((n_seq, H_B // 2, LANES, LANES), lambda b_, c: (b_, 0, 0, 0))],
        out_shape=[jax.ShapeDtypeStruct((nseq, s, W_B), BF16),
                   jax.ShapeDtypeStruct((nseq, H_B // 2, LANES, LANES), F32)],
        scratch_shapes=[pltpu.VMEM((n_seq, 1, N_RWKV), F32),
                        pltpu.VMEM((n_seq * (H_B // 2), LANES, LANES), F32)],
        compiler_params=_params("parallel", "arbitrary"),
        name="rwkv7_chunk",
    )(prw, shift.reshape(nseq, 1, N_RWKV), _pair_states(wkv), mu.reshape(1, N_RWKV), wwa_hi, wwa_lo, g2_hi, g2_lo,
      vecs, bd)
    return y, _unpair_states(s_out)


def _merge_kernel(x_ref, mod_ref, ya_ref, yb_ref, wg_ref, bg_ref, wa_ref, wb_ref, wo_ref, g_ref, b_ref,
                  o_ref, *, alpha):
    g_, ts, d = x_ref.shape
    m = g_ * ts
    x = x_ref[...]
    shift, scale, gate = _mod_rows(mod_ref, 1)
    u = (_norm(x) * (1.0 + scale) + shift).reshape(m, d).astype(BF16)
    ma = jnp.dot(ya_ref[...].reshape(m, -1), wa_ref[...], preferred_element_type=F32)
    mb = jnp.dot(yb_ref[...].reshape(m, -1), wb_ref[...], preferred_element_type=F32)
    ga = _sigmoid(jnp.dot(u, wg_ref[:, :d], preferred_element_type=F32) + bg_ref[:, :d])
    gb = _sigmoid(jnp.dot(u, wg_ref[:, d:], preferred_element_type=F32) + bg_ref[:, d:])
    merged = (ga * ma + gb * mb).astype(BF16)
    z = jnp.dot(merged, wo_ref[...], preferred_element_type=F32).reshape(g_, ts, d)
    o_ref[...] = _norm(alpha * x + gate * z) * g_ref[...] + b_ref[...]


def _merge_call(x, mod, ya, yb, wg, bg, wa, wb, wo, ln_g, ln_b, alpha):
    nseq, s, d = x.shape
    g_, ts = _seq_tiling(nseq, s)
    const = lambda i, j: (0, 0)
    blk = lambda w: pl.BlockSpec((g_, ts, w), lambda i, j: (i, j, 0))
    full = lambda a: pl.BlockSpec(a.shape, const, pipeline_mode=pl.Buffered(1))
    return pl.pallas_call(
        functools.partial(_merge_kernel, alpha=alpha),
        grid=(nseq // g_, s // ts),
        in_specs=[blk(d), pl.BlockSpec((g_, 9, d), lambda i, j: (i, 0, 0)), blk(ya.shape[-1]), blk(yb.shape[-1]),
                  full(wg), pl.BlockSpec((1, 2 * d), const), full(wa), full(wb), full(wo),
                  pl.BlockSpec((1, d), const), pl.BlockSpec((1, d), const)],
        out_specs=blk(d),
        out_shape=jax.ShapeDtypeStruct(x.shape, F32),
        compiler_params=_params("parallel", "parallel"),
        name="merge_out",
    )(x, mod, ya, yb, wg, bg.reshape(1, 2 * d), wa, wb, wo, ln_g.reshape(1, d), ln_b.reshape(1, d))


def _layer(x, mod, lw, rel_bias, caches, wkv, shift, lam_init, alpha, layer, depth, stacks):
    nseq, s, _ = x.shape
    x = _ffn_call(x, mod, lw["w_gu0"], lw["w_down0"], lw["ln_g"][0], lw["ln_b"][0], 0, alpha)
    q, k_stack, v_stack, kb, vb, prw = _mixin_call(x, mod, lw["w_in"], layer, depth, stacks)
    if caches is None:
        ya = _attn_call(q, kb, vb, rel_bias, lw["lam_qk"], lw["subln_g"], lam_init)
    else:
        ya = _attn_cached_call(q, caches[0], caches[1], layer, kb, vb, rel_bias, lw["lam_qk"], lw["subln_g"],
                               lam_init)
    yb, wkv_new = _rwkv_call(prw, shift, wkv, lw["rw_mu"], lw["rw_wwa"], lw["rw_g2"], lw["rw_vecs"])
    x = _merge_call(x, mod, ya, yb, lw["w_gate"], lw["b_gate"], lw["w_br_a"], lw["w_br_b"], lw["w_o"],
                    lw["ln_g"][1], lw["ln_b"][1], alpha)
    x = _ffn_call(x, mod, lw["w_gu1"], lw["w_down1"], lw["ln_g"][2], lw["ln_b"][2], 2, alpha)
    return x, (k_stack, v_stack), wkv_new, prw[:, -1]


def kernel(x_prompt, x_sample, c_prompt, c_sample, cache_k, cache_v, state_wkv, state_shift, rel_bias, w_ada, b_ada, ln_g, ln_b, w_gu, w_down, w_in, lam_qk, subln_g, rw_mu, rw_w0, rw_w2, rw_a0, rw_a2, rw_g2, rw_k_k, rw_k_a, rw_r_k, rw_lnx_g, rw_lnx_b, w_br_a, w_br_b, w_gate, b_gate, w_o):
    depth = w_ada.shape[0]
    b_p = x_prompt.shape[0]
    alpha = (2 * depth) ** 0.25
    mod = _ada_call(jnp.concatenate([c_prompt, c_sample], axis=0), w_ada, b_ada)
    mod = mod.reshape(depth, -1, 9, D_MODEL)

    w_gu_b, w_down_b, w_in_b = w_gu.astype(BF16), w_down.astype(BF16), w_in.astype(BF16)
    w_gate_b, w_br_a_b, w_br_b_b, w_o_b = (w.astype(BF16) for w in (w_gate, w_br_a, w_br_b, w_o))
    zeros_l = jnp.zeros((depth, W_LORA, W_B), F32)
    rw_wwa = jnp.concatenate([jnp.concatenate([rw_w2, zeros_l], axis=2),
                              jnp.concatenate([zeros_l, rw_a2], axis=2)], axis=1)
    rw_vecs = jnp.stack([rw_w0, rw_a0, rw_k_k, rw_k_a, rw_r_k.reshape(depth, W_B), rw_lnx_g, rw_lnx_b,
                         jnp.zeros_like(rw_w0)], axis=1)

    zero_wkv = jnp.zeros((b_p, H_B, DH_B, DH_B), F32)
    zero_shift = jnp.zeros((b_p, N_RWKV), F32)
    hp, hd = x_prompt, x_sample
    kv_p = kv_d = None
    wkv_p, wkv_d, shift_p, shift_d = [], [], [], []
    for l in range(depth):
        lw = {"w_gu0": w_gu_b[l, 0], "w_gu1": w_gu_b[l, 1], "w_down0": w_down_b[l, 0], "w_down1": w_down_b[l, 1],
              "w_in": w_in_b[l], "ln_g": ln_g[l], "ln_b": ln_b[l], "lam_qk": lam_qk[l], "subln_g": subln_g[l],
              "rw_mu": rw_mu[l], "rw_wwa": rw_wwa[l], "rw_g2": rw_g2[l], "rw_vecs": rw_vecs[l],
              "w_gate": w_gate_b[l], "b_gate": b_gate[l], "w_br_a": w_br_a_b[l], "w_br_b": w_br_b_b[l],
              "w_o": w_o_b[l]}
        lam_init = 0.8 - 0.6 * math.exp(-0.3 * l)
        hp, kv_p, wkv, shift = _layer(hp, mod[l, :b_p], lw, rel_bias, None, zero_wkv, zero_shift,
                                      lam_init, alpha, l, depth, kv_p)
        wkv_p.append(wkv)
        shift_p.append(shift)
        hd, kv_d, wkv, shift = _layer(hd, mod[l, b_p:], lw, rel_bias, (cache_k, cache_v), state_wkv[l],
                                      state_shift[l], lam_init, alpha, l, depth, kv_d)
        wkv_d.append(wkv)
        shift_d.append(shift)
    return (hp, hd, kv_p[0], kv_p[1], jnp.stack(wkv_p), jnp.stack(shift_p),
            kv_d[0], kv_d[1], jnp.stack(wkv_d), jnp.stack(shift_d))
```

```python
import functools
import math

import jax
import jax.numpy as jnp
from jax import lax
from jax.experimental import pallas as pl
from jax.experimental.pallas import tpu as pltpu

F32 = jnp.float32
BF16 = jnp.bfloat16

D_MODEL = 1024
CHUNK = 64
H_A = 4
DH_A = 64
DV_A = 128
N_QK = 512
N_ATT = 1536
H_B = 8
DH_B = 64
W_B = 512
W_LORA = 64
A_LORA = 64
G_LORA = 128
N_RWKV = 1792
D_FF = 2816
N_BUCKETS = 32
LN_EPS = 1e-5
GN_EPS = 64e-5

LANES = 128
VMEM_LIMIT = 56 * 1024 * 1024
NEG_BIG = -1e30
FAR_BUCKET = N_BUCKETS // 2 - 1
BUCKET_STARTS = (12, 16, 23, 32, 46, 64, 91)

FF_CHUNK = 256
ATTN_TILE = 512
ROW_GROUPS = 4
LOG2E = 1.4426950408889634
_NN = (((1,), (0,)), ((), ()))
_NT = (((1,), (1,)), ((), ()))
_TN = (((0,), (0,)), ((), ()))
SOLVE_BLOCK = 16


def _params(*sem):
    return pltpu.CompilerParams(dimension_semantics=sem, vmem_limit_bytes=VMEM_LIMIT)


def _norm(x):
    mu = jnp.mean(x, axis=-1, keepdims=True)
    xc = x - mu
    var = jnp.mean(xc * xc, axis=-1, keepdims=True)
    return xc * lax.rsqrt(var + LN_EPS)


def _sigmoid(x):
    return 1.0 / (1.0 + jnp.exp(-x))


def _div_pow2(x, n):
    assert n & (n - 1) == 0
    return lax.shift_right_logical(x, n.bit_length() - 1)


def _mod_rows(mod_ref, sub):
    return tuple(mod_ref[:, 3 * sub + j:3 * sub + j + 1, :] for j in range(3))


def _ada_kernel(c_ref, w_ref, b_ref, o_ref):
    c = c_ref[...]
    sc = (c * _sigmoid(c)).astype(BF16)
    o_ref[0] = jnp.dot(sc, w_ref[0].astype(BF16), preferred_element_type=F32) + b_ref[0]


def _ada_call(c, w_ada, b_ada):
    depth, d, n = w_ada.shape
    nseq = c.shape[0]
    tn = 1152
    return pl.pallas_call(
        _ada_kernel,
        grid=(depth, n // tn),
        in_specs=[pl.BlockSpec((nseq, d), lambda l, j: (0, 0)),
                  pl.BlockSpec((1, d, tn), lambda l, j: (l, 0, j)),
                  pl.BlockSpec((1, 1, tn), lambda l, j: (l, 0, j))],
        out_specs=pl.BlockSpec((1, nseq, tn), lambda l, j: (l, 0, j)),
        out_shape=jax.ShapeDtypeStruct((depth, nseq, n), F32),
        compiler_params=_params("parallel", "parallel"),
        name="ada_mod",
    )(c, w_ada, b_ada.reshape(depth, 1, n))


def _ffn_kernel(x_ref, mod_ref, wgu_ref, wd_ref, g_ref, b_ref, o_ref, *, sub, fc, alpha):
    g_, ts, d = x_ref.shape
    x = x_ref[...]
    shift, scale, gate = _mod_rows(mod_ref, sub)
    u = (_norm(x) * (1.0 + scale) + shift).reshape(g_ * ts, d).astype(BF16)
    f = jnp.zeros((g_ * ts, d), F32)
    for c in range(D_FF // fc):
        hg = jnp.dot(u, wgu_ref[:, c * fc:(c + 1) * fc], preferred_element_type=F32)
        hu = jnp.dot(u, wgu_ref[:, D_FF + c * fc:D_FF + (c + 1) * fc], preferred_element_type=F32)
        act = (hg * _sigmoid(hg) * hu).astype(BF16)
        f = f + jnp.dot(act, wd_ref[c * fc:(c + 1) * fc, :], preferred_element_type=F32)
    y = alpha * x + (0.5 * gate) * f.reshape(g_, ts, d)
    o_ref[...] = _norm(y) * g_ref[...] + b_ref[...]


def _seq_tiling(nseq, s):
    if s >= 512:
        return 1, 512
    g = max(1, min(nseq, 256 // s))
    while nseq % g:
        g -= 1
    return g, s


def _ffn_call(x, mod, wgu, wd, ln_g, ln_b, sub, alpha):
    nseq, s, d = x.shape
    g_, ts = _seq_tiling(nseq, s)
    const = lambda i, j: (0, 0)
    return pl.pallas_call(
        functools.partial(_ffn_kernel, sub=sub, fc=FF_CHUNK, alpha=alpha),
        grid=(nseq // g_, s // ts),
        in_specs=[pl.BlockSpec((g_, ts, d), lambda i, j: (i, j, 0)),
                  pl.BlockSpec((g_, 9, d), lambda i, j: (i, 0, 0)),
                  pl.BlockSpec(wgu.shape, const, pipeline_mode=pl.Buffered(1)),
                  pl.BlockSpec(wd.shape, const, pipeline_mode=pl.Buffered(1)),
                  pl.BlockSpec((1, d), const),
                  pl.BlockSpec((1, d), const)],
        out_specs=pl.BlockSpec((g_, ts, d), lambda i, j: (i, j, 0)),
        out_shape=jax.ShapeDtypeStruct(x.shape, F32),
        compiler_params=_params("parallel", "parallel"),
        name="ffn_block",
    )(x, mod, wgu, wd, ln_g.reshape(1, d), ln_b.reshape(1, d))


def _mixin_kernel(x_ref, mod_ref, win_ref, *refs, slab):
    q_ref, k_ref, v_ref, kb_ref, vb_ref, prw_ref = refs[-6:]
    g_, ts, d = x_ref.shape
    shift, scale, _ = _mod_rows(mod_ref, 1)
    u = (_norm(x_ref[...]) * (1.0 + scale) + shift).reshape(g_ * ts, d).astype(BF16)

    def proj(lo, hi):
        return jnp.dot(u, win_ref[:, lo:hi], preferred_element_type=F32).reshape(g_, ts, hi - lo)

    q = proj(0, N_QK)
    q_ref[...] = (q * (DH_A ** -0.5 * LOG2E)).astype(BF16)
    k = proj(N_QK, 2 * N_QK)
    kb_ref[...] = k.astype(BF16)
    v = proj(2 * N_QK, N_ATT)
    vb_ref[...] = v.astype(BF16)
    for h in range(H_A):
        k_ref[slab, :, :, h, :] = k[:, :, h * LANES:(h + 1) * LANES]
        v_ref[slab, :, :, h, :] = v[:, :, h * LANES:(h + 1) * LANES]
    for other in range(k_ref.shape[0]):
        if other != slab:
            k_ref[other] = jnp.zeros(k_ref.shape[1:], F32)
            v_ref[other] = jnp.zeros(v_ref.shape[1:], F32)
    prw_ref[...] = proj(N_ATT, N_ATT + N_RWKV)


def _mixin_call(x, mod, win, layer, depth, stacks):
    nseq, s, d = x.shape
    g_, ts = _seq_tiling(nseq, s)
    const = lambda i, j: (0, 0)
    blk = lambda w: pl.BlockSpec((g_, ts, w), lambda i, j: (i, j, 0))
    sds = lambda w, dt: jax.ShapeDtypeStruct((nseq, s, w), dt)
    if stacks is None:
        slab, slab_index = pl.BlockSpec((depth, g_, ts, H_A, LANES), lambda i, j: (0, i, j, 0, 0)), layer
    else:
        slab, slab_index = pl.BlockSpec((1, g_, ts, H_A, LANES), lambda i, j: (layer, i, j, 0, 0)), 0
    stack_sds = jax.ShapeDtypeStruct((depth, nseq, s, H_A, LANES), F32)
    in_specs = [blk(d),
                pl.BlockSpec((g_, 9, d), lambda i, j: (i, 0, 0)),
                pl.BlockSpec(win.shape, const, pipeline_mode=pl.Buffered(1))]
    args, aliases = [x, mod, win], {}
    if stacks is not None:
        in_specs += [pl.BlockSpec(memory_space=pl.ANY)] * 2
        args += list(stacks)
        aliases = {3: 1, 4: 2}
    return pl.pallas_call(
        functools.partial(_mixin_kernel, slab=slab_index),
        grid=(nseq // g_, s // ts),
        in_specs=in_specs,
        out_specs=[blk(N_QK), slab, slab, blk(N_QK), blk(N_QK), blk(N_RWKV)],
        out_shape=[sds(N_QK, BF16), stack_sds, stack_sds, sds(N_QK, BF16), sds(N_QK, BF16), sds(N_RWKV, F32)],
        input_output_aliases=aliases,
        compiler_params=_params("parallel", "parallel"),
        name="mixer_in",
    )(*args)


def _bias_tile(rb_ref, h, tq, tk, offset, valid_cols):
    r = lax.broadcasted_iota(jnp.int32, (tq, tk), 0)
    c = lax.broadcasted_iota(jnp.int32, (tq, tk), 1)
    rel = c - r + offset
    n = jnp.abs(rel)
    log_bucket = jnp.full((tq, tk), N_BUCKETS // 4, jnp.int32)
    for start in BUCKET_STARTS:
        log_bucket = log_bucket + jnp.where(n >= start, 1, 0)
    bucket = jnp.where(rel > 0, N_BUCKETS // 2, 0) + jnp.where(n < N_BUCKETS // 4, n, log_bucket)
    bias = jnp.zeros((tq, tk), F32)
    for b in range(N_BUCKETS):
        bias = jnp.where(bucket == b, rb_ref[b * H_A + h], bias)
    if offset == 0:
        visible = jnp.logical_and(_div_pow2(c, CHUNK) <= _div_pow2(r, CHUNK), c < valid_cols)
        bias = jnp.where(visible, bias, NEG_BIG)
    return bias


def _stack_maps(q):
    lane = lax.broadcasted_iota(jnp.int32, q.shape, 1)
    zero = jnp.zeros_like(q)
    return jnp.concatenate([jnp.where(lane < DH_A, q, zero), jnp.where(lane < DH_A, zero, q)], axis=0)


def _diff_head_out(o_all, tq, lam_ref, g_ref, lam_init):
    lq = lam_ref[...]
    lam = (jnp.exp(jnp.sum(lq[0:1] * lq[1:2], axis=1, keepdims=True))
           - jnp.exp(jnp.sum(lq[2:3] * lq[3:4], axis=1, keepdims=True)) + lam_init)
    o = o_all[:tq] - lam * o_all[tq:]
    o = o * lax.rsqrt(jnp.mean(o * o, axis=-1, keepdims=True) + LN_EPS)
    return o * g_ref[...] * (1.0 - lam_init)


def _attn_kernel(rb_ref, q_ref, k_ref, v_ref, lam_ref, g_ref, o_ref, m_scr, acc_scr, bias_scr,
                 *, tq, tk, lam_init):
    h = pl.program_id(1)
    i = pl.program_id(2)

    @pl.when(i == 0)
    def _():
        for slot, offset in ((0, -tk), (1, 0)):
            bias_scr[slot] = _bias_tile(rb_ref, h, tq, tk, offset, tk) * LOG2E

    q2 = _stack_maps(q_ref[0])
    m_scr[...] = jnp.full(m_scr.shape, NEG_BIG, F32)
    acc_scr[...] = jnp.zeros(acc_scr.shape, F32)
    ones = jnp.ones((tk, LANES), BF16)

    def tiles(js, bias_slots, shift):
        rows = 2 * tq // ROW_GROUPS
        starts = [pl.multiple_of(j * tk, tk) for j in js]

        def score(start, g):
            return lax.dot_general(q2[g * rows:(g + 1) * rows], k_ref[0, pl.ds(start, tk), :], _NT,
                                   preferred_element_type=F32)

        def update(start_v1, slot, g, s):
            sl = slice(g * rows, (g + 1) * rows)
            if slot is not None:
                r0 = (g * rows) % tq
                s = s + bias_scr[slot, r0:r0 + rows, :]
            m_prev = m_scr[sl]
            m_new = jnp.maximum(m_prev, jnp.max(s, axis=1, keepdims=True) + shift)
            p = jnp.exp2(s - jnp.tile(m_new - shift, (1, tk // LANES)))
            alpha = jnp.exp2(m_prev - m_new)
            acc_scr[sl] = (jnp.tile(alpha, (1, 2)) * acc_scr[sl]
                           + jnp.dot(p.astype(BF16), start_v1, preferred_element_type=F32))
            m_scr[sl] = m_new

        scores = [score(starts[0], g) for g in range(ROW_GROUPS)]
        for t, (start, slot) in enumerate(zip(starts, bias_slots)):
            v1 = jnp.concatenate([v_ref[0, pl.ds(start, tk), :], ones], axis=1)
            nxt = []
            for g in range(ROW_GROUPS):
                if t + 1 < len(starts):
                    nxt.append(score(starts[t + 1], g))
                update(v1, slot, g, scores[g])
            scores = nxt

    far_shift = rb_ref[FAR_BUCKET * H_A + h] * LOG2E
    n_far = jnp.maximum(i - 1, 0)

    def far_body(jj, carry):
        tiles([2 * jj, 2 * jj + 1], [None, None], far_shift)
        return carry

    lax.fori_loop(0, n_far // 2, far_body, 0)

    @pl.when(n_far % 2 == 1)
    def _():
        tiles([n_far - 1], [None], far_shift)

    @pl.when(i >= 1)
    def _():
        tiles([i - 1, i], [0, 1], 0.0)

    @pl.when(i == 0)
    def _():
        tiles([i], [1], 0.0)

    acc = acc_scr[...]
    o_ref[0] = _diff_head_out(acc[:, :DV_A] / acc[:, DV_A:], tq, lam_ref, g_ref, lam_init).astype(o_ref.dtype)


def _attn_call(q, k, v, rel_bias, lam_qk, subln_g, lam_init):
    b, s, _ = q.shape
    tq = tk = ATTN_TILE
    assert s % tk == 0 and tk % CHUNK == 0 and (2 * tq) % ROW_GROUPS == 0 and tq % (2 * tq // ROW_GROUPS) == 0
    kern = functools.partial(_attn_kernel, tq=tq, tk=tk, lam_init=lam_init)
    return pl.pallas_call(
        kern,
        grid=(b, H_A, s // tq),
        in_specs=[pl.BlockSpec(memory_space=pltpu.SMEM),
                  pl.BlockSpec((1, tq, LANES), lambda b_, h, i: (b_, i, h)),
                  pl.BlockSpec((1, s, LANES), lambda b_, h, i: (b_, 0, h)),
                  pl.BlockSpec((1, s, LANES), lambda b_, h, i: (b_, 0, h)),
                  pl.BlockSpec((4, DH_A), lambda b_, h, i: (0, 0)),
                  pl.BlockSpec((1, DV_A), lambda b_, h, i: (0, 0))],
        out_specs=pl.BlockSpec((1, tq, LANES), lambda b_, h, i: (b_, i, h)),
        out_shape=jax.ShapeDtypeStruct((b, s, H_A * DV_A), BF16),
        scratch_shapes=[pltpu.VMEM((2 * tq, LANES), F32), pltpu.VMEM((2 * tq, 2 * DV_A), F32),
                        pltpu.VMEM((2, tq, tk), F32)],
        compiler_params=_params("parallel", "parallel", "arbitrary"),
        name="diff_attn",
    )(rel_bias.reshape(-1), q, k, v, lam_qk, subln_g.reshape(1, DV_A))


def _attn_cached_kernel(rb_ref, q_ref, kc_ref, vc_ref, kn_ref, vn_ref, lam_ref, g_ref, o_ref,
                        *, sq, past, lam_init):
    near = min(past, LANES)
    for h in range(H_A):
        lanes = slice(h * LANES, (h + 1) * LANES)
        q2 = _stack_maps(q_ref[0, :, lanes])
        far = jnp.full((sq, past - near), rb_ref[FAR_BUCKET * H_A + h], F32)
        bias_p = jnp.concatenate([far, _bias_tile(rb_ref, h, sq, near, -near, near)], axis=1) * LOG2E
        bias_n = _bias_tile(rb_ref, h, sq, sq, 0, sq) * LOG2E
        kp = kc_ref[0, 0, pl.ds(h, past, stride=H_A), :].astype(BF16)
        vp = vc_ref[0, 0, pl.ds(h, past, stride=H_A), :].astype(BF16)
        sp = lax.dot_general(q2, kp, _NT, preferred_element_type=F32)
        sn = lax.dot_general(q2, kn_ref[0, :, lanes], _NT, preferred_element_type=F32)
        sp = (sp.reshape(2, sq, past) + bias_p).reshape(2 * sq, past)
        sn = (sn.reshape(2, sq, sq) + bias_n).reshape(2 * sq, sq)
        m = jnp.maximum(jnp.max(sp, axis=1, keepdims=True), jnp.max(sn, axis=1, keepdims=True))
        pp = jnp.exp2(sp - m)
        pn = jnp.exp2(sn - m)
        l = jnp.sum(pp, axis=1, keepdims=True) + jnp.sum(pn, axis=1, keepdims=True)
        acc = (jnp.dot(pp.astype(BF16), vp, preferred_element_type=F32)
               + jnp.dot(pn.astype(BF16), vn_ref[0, :, lanes], preferred_element_type=F32))
        o_ref[0, :, lanes] = _diff_head_out(acc / l, sq, lam_ref, g_ref, lam_init).astype(o_ref.dtype)


def _attn_cached_call(q, cache_k, cache_v, layer, k_new, v_new, rel_bias, lam_qk, subln_g, lam_init):
    b, sq, _ = q.shape
    depth, _, past = cache_k.shape[:3]
    assert past % CHUNK == 0 and past >= LANES and sq % 8 == 0
    kern = functools.partial(_attn_cached_kernel, sq=sq, past=past, lam_init=lam_init)
    new = pl.BlockSpec((1, sq, H_A * LANES), lambda b_: (b_, 0, 0))
    slab = pl.BlockSpec((1, 1, past * H_A, LANES), lambda b_: (layer, b_, 0, 0))
    rows = lambda c: c.reshape(depth, b, past * H_A, LANES)
    return pl.pallas_call(
        kern,
        grid=(b,),
        in_specs=[pl.BlockSpec(memory_space=pltpu.SMEM), new, slab, slab, new, new,
                  pl.BlockSpec((4, DH_A), lambda b_: (0, 0)),
                  pl.BlockSpec((1, DV_A), lambda b_: (0, 0))],
        out_specs=new,
        out_shape=jax.ShapeDtypeStruct((b, sq, H_A * DV_A), BF16),
        compiler_params=_params("parallel"),
        name="diff_attn_cached",
    )(rel_bias.reshape(-1), q, rows(cache_k), rows(cache_v), k_new, v_new, lam_qk, subln_g.reshape(1, DV_A))


def _split(x):
    hi = x.astype(BF16)
    return hi, (x - hi.astype(F32)).astype(BF16)


def _cat(pairs, axis):
    return tuple(jnp.concatenate(parts, axis=axis) for parts in zip(*pairs))


def _pdot(a, b, dims=_NN):
    (ah, al), (bh, bl) = a, b
    ca, cb = dims[0][0][0], dims[0][1][0]
    two = lax.dot_general(jnp.concatenate([ah, al], axis=ca), jnp.concatenate([bh, bh], axis=cb), dims,
                          preferred_element_type=F32)
    return two + lax.dot_general(ah, bl, dims, preferred_element_type=F32)


def _pdot_rhs16(a, b, dims=_NN):
    (ah, al), (bh, _) = a, b
    ca, cb = dims[0][0][0], dims[0][1][0]
    return lax.dot_general(jnp.concatenate([ah, al], axis=ca), jnp.concatenate([bh, bh], axis=cb), dims,
                           preferred_element_type=F32)


def _pdot_ones_rhs(a, ones):
    return jnp.dot(jnp.concatenate(a, axis=1), jnp.concatenate([ones, ones], axis=0),
                   preferred_element_type=F32)


def _pdot_ones_lhs(ones, b):
    return jnp.dot(jnp.concatenate([ones, ones], axis=1), jnp.concatenate(b, axis=0),
                   preferred_element_type=F32)


def _rwkv_kernel(prw_ref, shift_ref, s0_ref, mu_ref, wwa_hi_ref, wwa_lo_ref, g2_hi_ref, g2_lo_ref, vec_ref,
                 bd_ref, y_ref, sout_ref, carry_scr, state_scr, *, c_len):
    ci = pl.program_id(1)
    nc = pl.num_programs(1)
    n_pairs = H_B // 2
    n_seq = prw_ref.shape[0]

    @pl.when(ci == 0)
    def _():
        carry_scr[...] = shift_ref[...]
        state_scr[...] = s0_ref[...].reshape(state_scr.shape)

    row = lax.broadcasted_iota(jnp.int32, (c_len, N_RWKV), 0)
    xm = []
    for sq_i in range(n_seq):
        p = prw_ref[sq_i]
        prev = jnp.where(row == 0, carry_scr[sq_i], pltpu.roll(p, 1, axis=0))
        carry_scr[sq_i] = p[c_len - 1:c_len]
        xm.append(p + (prev - p) * mu_ref[...])
    xm = jnp.concatenate(xm, axis=0)

    w0, a0, k_k, k_a, r_k, lnx_g, lnx_b = (vec_ref[j:j + 1] for j in range(7))
    r = xm[:, :W_B]
    k = xm[:, W_B:2 * W_B]
    v = xm[:, 2 * W_B:3 * W_B]
    lora_in = xm[:, 3 * W_B:3 * W_B + W_LORA + A_LORA]
    lane = lax.broadcasted_iota(jnp.int32, lora_in.shape, 1)
    lora = _pdot(_split(jnp.where(lane < W_LORA, jnp.tanh(lora_in), lora_in)),
                 (wwa_hi_ref[...], wwa_lo_ref[...]))
    ww = -(w0 + lora[:, :W_B])
    w_log = -(jnp.maximum(ww, 0.0) + jnp.log(1.0 + jnp.exp(-jnp.abs(ww)))) - 0.5
    lw = -jnp.exp(w_log)
    a = _sigmoid(a0 + lora[:, W_B:])
    g = _pdot(_split(_sigmoid(xm[:, 3 * W_B + W_LORA + A_LORA:])), (g2_hi_ref[...], g2_lo_ref[...]))

    bd = bd_ref[...]

    def head_sum(x):
        return jnp.concatenate([_pdot_ones_rhs(_split(x[:, j * LANES:(j + 1) * LANES]), bd)
                                for j in range(W_B // LANES)], axis=1)

    kk = k * k_k
    kk = kk * lax.rsqrt(jnp.maximum(head_sum(kk * kk), 1e-24))
    km = k * (1.0 + (a - 1.0) * k_a)
    kb = kk * a
    bonus = head_sum(r * km * r_k) * v

    c2 = 2 * c_len
    tri_r = lax.broadcasted_iota(jnp.int32, (c_len, c_len), 0)
    tri_c = lax.broadcasted_iota(jnp.int32, (c_len, c_len), 1)
    tril = jnp.where(tri_r >= tri_c, 1.0, 0.0).astype(BF16)
    rr = lax.broadcasted_iota(jnp.int32, (c2, c2), 0)
    cc = lax.broadcasted_iota(jnp.int32, (c2, c2), 1)
    strict = rr > cc
    incl = rr >= cc
    eye = rr == cc
    same_blk = _div_pow2(rr, SOLVE_BLOCK) == _div_pow2(cc, SOLVE_BLOCK)
    head0 = lax.broadcasted_iota(jnp.int32, (c_len, LANES), 1) < DH_B

    def stack(x):
        z = jnp.zeros_like(x)
        return jnp.concatenate([jnp.where(head0, x, z), jnp.where(head0, z, x)], axis=0)

    def plus_eye(x):
        return jnp.where(eye, x + 1.0, x)

    def par(fn, *per_pair):
        return [fn(*args) for args in zip(*per_pair)]

    pairs = [(slice(i * c_len, (i + 1) * c_len), slice(j * LANES, (j + 1) * LANES))
             for i in range(n_seq) for j in range(n_pairs)]
    zero = jnp.zeros((c2, c2), F32)
    lw_p = [lw[sl] for sl in pairs]
    cum = par(lambda x: _pdot_ones_lhs(tril, _split(x)), lw_p)
    g_in = par(jnp.exp, cum)
    g_inv = par(lambda x: jnp.exp(-x), cum)
    a_hat = par(lambda sl, x, y: _split(stack(-kk[sl] * jnp.exp(x - y))), pairs, cum, lw_p)
    r_hat = par(lambda sl, x: _split(stack(r[sl] * x)), pairs, g_in)
    b_til = par(lambda sl, x: _split(stack(kb[sl] * x)), pairs, g_inv)
    k_til = par(lambda sl, x: _split(stack(km[sl] * x)), pairs, g_inv)
    v2 = par(lambda sl: _split(stack(v[sl])), pairs)
    lhs = par(lambda x, y: _cat([x, y], 0), a_hat, r_hat)
    if c2 % LANES == 0:
        both = par(lambda x, y, z: _pdot_rhs16(x, _cat([y, z], 0), _NT), lhs, b_til, k_til)
        ab_rb = [x[:, :c2] for x in both]
        ak_rk = [x[:, c2:] for x in both]
    else:
        ab_rb = par(lambda x, y: _pdot_rhs16(x, y, _NT), lhs, b_til)
        ak_rk = par(lambda x, y: _pdot_rhs16(x, y, _NT), lhs, k_til)
    a_ab = par(lambda x: jnp.where(strict, x[:c2], zero), ab_rb)
    a_rb = par(lambda x: _split(jnp.where(incl, x[c2:], zero)), ab_rb)
    a_ak = par(lambda x: _split(jnp.where(strict, x[:c2], zero)), ak_rk)
    a_rk = par(lambda x: _split(jnp.where(incl, x[c2:], zero)), ak_rk)

    d1 = par(lambda x: jnp.where(same_blk, x, zero), a_ab)
    dinv = par(plus_eye, d1)
    dpow = par(_split, d1)
    for _ in range(int(math.log2(SOLVE_BLOCK)) - 1):
        dpow = par(lambda x: _split(_pdot(x, x)), dpow)
        dinv = par(lambda x, y: x + _pdot(_split(x), y), dinv, dpow)
    npow = par(lambda x, y, z: _split(_pdot(_split(x), _split(y - z))), dinv, a_ab, d1)
    t_mat = dinv
    n_levels = max(1, int(math.ceil(math.log2(c_len // SOLVE_BLOCK))))
    for level in range(n_levels):
        t_mat = par(lambda x, y: y + _pdot(x, _split(y)), npow, t_mat)
        if level + 1 < n_levels:
            npow = par(lambda x: _split(_pdot(x, x)), npow)
    t_mat = par(_split, t_mat)

    s_pair = [state_scr[j] for j in range(len(pairs))]
    s_split = par(_split, s_pair)
    rhs = par(lambda x, y, z, w: _split(_pdot_rhs16(x, y, _NT) + _pdot_rhs16(z, w)), a_hat, s_split, a_ak, v2)
    u = par(lambda x, y: _split(_pdot(x, y)), t_mat, rhs)
    o2 = par(lambda x, y, z, w, p_, q_: _pdot_rhs16(x, y, _NT) + _pdot_rhs16(z, w) + _pdot_rhs16(p_, q_),
             r_hat, s_split, a_rk, v2, a_rb, u)
    outs = par(lambda x: x[:c_len] + x[c_len:], o2)
    s_add = par(lambda x, y, z, w: _pdot_rhs16(_cat([x, y], 0), _cat([z, w], 0), _TN), v2, u, k_til, b_til)
    for j in range(len(pairs)):
        state_scr[j] = (s_pair[j] + s_add[j]) * g_in[j][c_len - 1:c_len]

    o = jnp.concatenate([jnp.concatenate(outs[i * n_pairs:(i + 1) * n_pairs], axis=1)
                         for i in range(n_seq)], axis=0)
    inv_dh = 1.0 / DH_B
    mean = head_sum(o) * inv_dh
    oc = o - mean
    var = head_sum(oc * oc) * inv_dh
    on = oc * lax.rsqrt(var + GN_EPS) * lnx_g + lnx_b
    y_ref[...] = ((on + bonus) * g).reshape(y_ref.shape).astype(y_ref.dtype)

    @pl.when(ci == nc - 1)
    def _():
        sout_ref[...] = state_scr[...].reshape(sout_ref.shape)


def _pair_states(wkv):
    b = wkv.shape[0]
    w = wkv.reshape(b, H_B // 2, 2, DH_B, DH_B)
    z = jnp.zeros_like(w[:, :, 0])
    top = jnp.concatenate([w[:, :, 0], z], axis=-1)
    bot = jnp.concatenate([z, w[:, :, 1]], axis=-1)
    return jnp.concatenate([top, bot], axis=-2)


def _unpair_states(sp):
    b = sp.shape[0]
    return jnp.stack([sp[:, :, :DH_B, :DH_B], sp[:, :, DH_B:, DH_B:]], axis=2).reshape(b, H_B, DH_B, DH_B)


def _rwkv_call(prw, shift, wkv, mu, wwa, g2, vecs):
    nseq, s, _ = prw.shape
    c_len = min(CHUNK, s)
    n_seq = 2 if nseq % 2 == 0 else 1
    assert s % c_len == 0 and c_len % SOLVE_BLOCK == 0
    const2 = lambda b_, c: (0, 0)
    head_of = jnp.arange(LANES, dtype=jnp.int32) // DH_B
    bd = (head_of[:, None] == head_of[None, :]).astype(BF16)
    wwa_hi, wwa_lo = _split(wwa)
    g2_hi, g2_lo = _split(g2)
    y, s_out = pl.pallas_call(
        functools.partial(_rwkv_kernel, c_len=c_len),
        grid=(nseq // n_seq, s // c_len),
        in_specs=[pl.BlockSpec((n_seq, c_len, N_RWKV), lambda b_, c: (b_, c, 0)),
                  pl.BlockSpec((n_seq, 1, N_RWKV), lambda b_, c: (b_, 0, 0)),
                  pl.BlockSpec((n_seq, H_B // 2, LANES, LANES), lambda b_, c: (b_, 0, 0, 0)),
                  pl.BlockSpec((1, N_RWKV), const2),
                  pl.BlockSpec(wwa.shape, const2),
                  pl.BlockSpec(wwa.shape, const2),
                  pl.BlockSpec(g2.shape, const2),
                  pl.BlockSpec(g2.shape, const2),
                  pl.BlockSpec(vecs.shape, const2),
                  pl.BlockSpec(bd.shape, const2)],
        out_specs=[pl.BlockSpec((n_seq, c_len, W_B), lambda b_, c: (b_, c, 0)),
                   pl.BlockSpec((n_seq, H_B // 2, LANES, LANES), lambda b_, c: (b_, 0, 0, 0))],
        out_shape=[jax.ShapeDtypeStruct((nseq, s, W_B), BF16),
                   jax.ShapeDtypeStruct((nseq, H_B // 2, LANES, LANES), F32)],
        scratch_shapes=[pltpu.VMEM((n_seq, 1, N_RWKV), F32),
                        pltpu.VMEM((n_seq * (H_B // 2), LANES, LANES), F32)],
        compiler_params=_params("parallel", "arbitrary"),
        name="rwkv7_chunk",
    )(prw, shift.reshape(nseq, 1, N_RWKV), _pair_states(wkv), mu.reshape(1, N_RWKV), wwa_hi, wwa_lo, g2_hi, g2_lo,
      vecs, bd)
    return y, _unpair_states(s_out)


def _merge_kernel(x_ref, mod_ref, ya_ref, yb_ref, wg_ref, bg_ref, wa_ref, wb_ref, wo_ref, g_ref, b_ref,
                  o_ref, *, alpha):
    g_, ts, d = x_ref.shape
    m = g_ * ts
    x = x_ref[...]
    shift, scale, gate = _mod_rows(mod_ref, 1)
    u = (_norm(x) * (1.0 + scale) + shift).reshape(m, d).astype(BF16)
    ma = jnp.dot(ya_ref[...].reshape(m, -1), wa_ref[...], preferred_element_type=F32)
    mb = jnp.dot(yb_ref[...].reshape(m, -1), wb_ref[...], preferred_element_type=F32)
    ga = _sigmoid(jnp.dot(u, wg_ref[:, :d], preferred_element_type=F32) + bg_ref[:, :d])
    gb = _sigmoid(jnp.dot(u, wg_ref[:, d:], preferred_element_type=F32) + bg_ref[:, d:])
    merged = (ga * ma + gb * mb).astype(BF16)
    z = jnp.dot(merged, wo_ref[...], preferred_element_type=F32).reshape(g_, ts, d)
    o_ref[...] = _norm(alpha * x + gate * z) * g_ref[...] + b_ref[...]


def _merge_call(x, mod, ya, yb, wg, bg, wa, wb, wo, ln_g, ln_b, alpha):
    nseq, s, d = x.shape
    g_, ts = _seq_tiling(nseq, s)
    const = lambda i, j: (0, 0)
    blk = lambda w: pl.BlockSpec((g_, ts, w), lambda i, j: (i, j, 0))
    full = lambda a: pl.BlockSpec(a.shape, const, pipeline_mode=pl.Buffered(1))
    return pl.pallas_call(
        functools.partial(_merge_kernel, alpha=alpha),
        grid=(nseq // g_, s // ts),
        in_specs=[blk(d), pl.BlockSpec((g_, 9, d), lambda i, j: (i, 0, 0)), blk(ya.shape[-1]), blk(yb.shape[-1]),
                  full(wg), pl.BlockSpec((1, 2 * d), const), full(wa), full(wb), full(wo),
                  pl.BlockSpec((1, d), const), pl.BlockSpec((1, d), const)],
        out_specs=blk(d),
        out_shape=jax.ShapeDtypeStruct(x.shape, F32),
        compiler_params=_params("parallel", "parallel"),
        name="merge_out",
    )(x, mod, ya, yb, wg, bg.reshape(1, 2 * d), wa, wb, wo, ln_g.reshape(1, d), ln_b.reshape(1, d))


def _layer(x, mod, lw, rel_bias, caches, wkv, shift, lam_init, alpha, layer, depth, stacks):
    nseq, s, _ = x.shape
    x = _ffn_call(x, mod, lw["w_gu0"], lw["w_down0"], lw["ln_g"][0], lw["ln_b"][0], 0, alpha)
    q, k_stack, v_stack, kb, vb, prw = _mixin_call(x, mod, lw["w_in"], layer, depth, stacks)
    if caches is None:
        ya = _attn_call(q, kb, vb, rel_bias, lw["lam_qk"], lw["subln_g"], lam_init)
    else:
        ya = _attn_cached_call(q, caches[0], caches[1], layer, kb, vb, rel_bias, lw["lam_qk"], lw["subln_g"],
                               lam_init)
    yb, wkv_new = _rwkv_call(prw, shift, wkv, lw["rw_mu"], lw["rw_wwa"], lw["rw_g2"], lw["rw_vecs"])
    x = _merge_call(x, mod, ya, yb, lw["w_gate"], lw["b_gate"], lw["w_br_a"], lw["w_br_b"], lw["w_o"],
                    lw["ln_g"][1], lw["ln_b"][1], alpha)
    x = _ffn_call(x, mod, lw["w_gu1"], lw["w_down1"], lw["ln_g"][2], lw["ln_b"][2], 2, alpha)
    return x, (k_stack, v_stack), wkv_new, prw[:, -1]


def kernel(x_prompt, x_sample, c_prompt, c_sample, cache_k, cache_v, state_wkv, state_shift, rel_bias, w_ada, b_ada, ln_g, ln_b, w_gu, w_down, w_in, lam_qk, subln_g, rw_mu, rw_w0, rw_w2, rw_a0, rw_a2, rw_g2, rw_k_k, rw_k_a, rw_r_k, rw_lnx_g, rw_lnx_b, w_br_a, w_br_b, w_gate, b_gate, w_o):
    depth = w_ada.shape[0]
    b_p = x_prompt.shape[0]
    alpha = (2 * depth) ** 0.25
    mod = _ada_call(jnp.concatenate([c_prompt, c_sample], axis=0), w_ada, b_ada)
    mod = mod.reshape(depth, -1, 9, D_MODEL)

    w_gu_b, w_down_b, w_in_b = w_gu.astype(BF16), w_down.astype(BF16), w_in.astype(BF16)
    w_gate_b, w_br_a_b, w_br_b_b, w_o_b = (w.astype(BF16) for w in (w_gate, w_br_a, w_br_b, w_o))
    zeros_l = jnp.zeros((depth, W_LORA, W_B), F32)
    rw_wwa = jnp.concatenate([jnp.concatenate([rw_w2, zeros_l], axis=2),
                              jnp.concatenate([zeros_l, rw_a2], axis=2)], axis=1)
    rw_vecs = jnp.stack([rw_w0, rw_a0, rw_k_k, rw_k_a, rw_r_k.reshape(depth, W_B), rw_lnx_g, rw_lnx_b,
                         jnp.zeros_like(rw_w0)], axis=1)

    zero_wkv = jnp.zeros((b_p, H_B, DH_B, DH_B), F32)
    zero_shift = jnp.zeros((b_p, N_RWKV), F32)
    hp, hd = x_prompt, x_sample
    kv_p = kv_d = None
    wkv_p, wkv_d, shift_p, shift_d = [], [], [], []
    for l in range(depth):
        lw = {"w_gu0": w_gu_b[l, 0], "w_gu1": w_gu_b[l, 1], "w_down0": w_down_b[l, 0], "w_down1": w_down_b[l, 1],
              "w_in": w_in_b[l], "ln_g": ln_g[l], "ln_b": ln_b[l], "lam_qk": lam_qk[l], "subln_g": subln_g[l],
              "rw_mu": rw_mu[l], "rw_wwa": rw_wwa[l], "rw_g2": rw_g2[l], "rw_vecs": rw_vecs[l],
              "w_gate": w_gate_b[l], "b_gate": b_gate[l], "w_br_a": w_br_a_b[l], "w_br_b": w_br_b_b[l],
              "w_o": w_o_b[l]}
        lam_init = 0.8 - 0.6 * math.exp(-0.3 * l)
        hp, kv_p, wkv, shift = _layer(hp, mod[l, :b_p], lw, rel_bias, None, zero_wkv, zero_shift,
                                      lam_init, alpha, l, depth, kv_p)
        wkv_p.append(wkv)
        shift_p.append(shift)
        hd, kv_d, wkv, shift = _layer(hd, mod[l, b_p:], lw, rel_bias, (cache_k, cache_v), state_wkv[l],
                                      state_shift[l], lam_init, alpha, l, depth, kv_d)
        wkv_d.append(wkv)
        shift_d.append(shift)
    return (hp, hd, kv_p[0], kv_p[1], jnp.stack(wkv_p), jnp.stack(shift_p),
            kv_d[0], kv_d[1], jnp.stack(wkv_d), jnp.stack(shift_d))
```

```python
import functools
import math

import jax
import jax.numpy as jnp
from jax import lax
from jax.experimental import pallas as pl
from jax.experimental.pallas import tpu as pltpu

F32 = jnp.float32
BF16 = jnp.bfloat16

D_MODEL = 1024
CHUNK = 64
H_A = 4
DH_A = 64
DV_A = 128
N_QK = 512
N_ATT = 1536
H_B = 8
DH_B = 64
W_B = 512
W_LORA = 64
A_LORA = 64
G_LORA = 128
N_RWKV = 1792
D_FF = 2816
N_BUCKETS = 32
LN_EPS = 1e-5
GN_EPS = 64e-5

LANES = 128
VMEM_LIMIT = 56 * 1024 * 1024
NEG_BIG = -1e30
FAR_BUCKET = N_BUCKETS // 2 - 1
BUCKET_STARTS = (12, 16, 23, 32, 46, 64, 91)

FF_CHUNK = 256
ATTN_TILE = 512
FAR_UNROLL = 4
ROW_GROUPS = 4
LOG2E = 1.4426950408889634
_NN = (((1,), (0,)), ((), ()))
_NT = (((1,), (1,)), ((), ()))
_TN = (((0,), (0,)), ((), ()))
SOLVE_BLOCK = 16


def _params(*sem):
    return pltpu.CompilerParams(dimension_semantics=sem, vmem_limit_bytes=VMEM_LIMIT)


def _norm(x):
    mu = jnp.mean(x, axis=-1, keepdims=True)
    xc = x - mu
    var = jnp.mean(xc * xc, axis=-1, keepdims=True)
    return xc * lax.rsqrt(var + LN_EPS)


def _sigmoid(x):
    return 1.0 / (1.0 + jnp.exp(-x))


def _div_pow2(x, n):
    assert n & (n - 1) == 0
    return lax.shift_right_logical(x, n.bit_length() - 1)


def _mod_rows(mod_ref, sub):
    return tuple(mod_ref[:, 3 * sub + j:3 * sub + j + 1, :] for j in range(3))


def _ada_kernel(c_ref, w_ref, b_ref, o_ref):
    c = c_ref[...]
    sc = (c * _sigmoid(c)).astype(BF16)
    o_ref[0] = jnp.dot(sc, w_ref[0].astype(BF16), preferred_element_type=F32) + b_ref[0]


def _ada_call(c, w_ada, b_ada):
    depth, d, n = w_ada.shape
    nseq = c.shape[0]
    tn = 1152
    return pl.pallas_call(
        _ada_kernel,
        grid=(depth, n // tn),
        in_specs=[pl.BlockSpec((nseq, d), lambda l, j: (0, 0)),
                  pl.BlockSpec((1, d, tn), lambda l, j: (l, 0, j)),
                  pl.BlockSpec((1, 1, tn), lambda l, j: (l, 0, j))],
        out_specs=pl.BlockSpec((1, nseq, tn), lambda l, j: (l, 0, j)),
        out_shape=jax.ShapeDtypeStruct((depth, nseq, n), F32),
        compiler_params=_params("parallel", "parallel"),
        name="ada_mod",
    )(c, w_ada, b_ada.reshape(depth, 1, n))


def _ffn_kernel(x_ref, mod_ref, wgu_ref, wd_ref, g_ref, b_ref, o_ref, *, sub, fc, alpha):
    g_, ts, d = x_ref.shape
    x = x_ref[...]
    shift, scale, gate = _mod_rows(mod_ref, sub)
    u = (_norm(x) * (1.0 + scale) + shift).reshape(g_ * ts, d).astype(BF16)
    f = jnp.zeros((g_ * ts, d), F32)
    for c in range(D_FF // fc):
        hg = jnp.dot(u, wgu_ref[:, c * fc:(c + 1) * fc], preferred_element_type=F32)
        hu = jnp.dot(u, wgu_ref[:, D_FF + c * fc:D_FF + (c + 1) * fc], preferred_element_type=F32)
        act = (hg * _sigmoid(hg) * hu).astype(BF16)
        f = f + jnp.dot(act, wd_ref[c * fc:(c + 1) * fc, :], preferred_element_type=F32)
    y = alpha * x + (0.5 * gate) * f.reshape(g_, ts, d)
    o_ref[...] = _norm(y) * g_ref[...] + b_ref[...]


def _seq_tiling(nseq, s):
    if s >= 512:
        return 1, 512
    g = max(1, min(nseq, 256 // s))
    while nseq % g:
        g -= 1
    return g, s


def _ffn_call(x, mod, wgu, wd, ln_g, ln_b, sub, alpha):
    nseq, s, d = x.shape
    g_, ts = _seq_tiling(nseq, s)
    const = lambda i, j: (0, 0)
    return pl.pallas_call(
        functools.partial(_ffn_kernel, sub=sub, fc=FF_CHUNK, alpha=alpha),
        grid=(nseq // g_, s // ts),
        in_specs=[pl.BlockSpec((g_, ts, d), lambda i, j: (i, j, 0)),
                  pl.BlockSpec((g_, 9, d), lambda i, j: (i, 0, 0)),
                  pl.BlockSpec(wgu.shape, const, pipeline_mode=pl.Buffered(1)),
                  pl.BlockSpec(wd.shape, const, pipeline_mode=pl.Buffered(1)),
                  pl.BlockSpec((1, d), const),
                  pl.BlockSpec((1, d), const)],
        out_specs=pl.BlockSpec((g_, ts, d), lambda i, j: (i, j, 0)),
        out_shape=jax.ShapeDtypeStruct(x.shape, F32),
        compiler_params=_params("parallel", "parallel"),
        name="ffn_block",
    )(x, mod, wgu, wd, ln_g.reshape(1, d), ln_b.reshape(1, d))


def _mixin_kernel(x_ref, mod_ref, win_ref, *refs, slab):
    q_ref, k_ref, v_ref, kb_ref, vb_ref, prw_ref = refs[-6:]
    g_, ts, d = x_ref.shape
    shift, scale, _ = _mod_rows(mod_ref, 1)
    u = (_norm(x_ref[...]) * (1.0 + scale) + shift).reshape(g_ * ts, d).astype(BF16)

    def proj(lo, hi):
        return jnp.dot(u, win_ref[:, lo:hi], preferred_element_type=F32).reshape(g_, ts, hi - lo)

    q = proj(0, N_QK)
    q_ref[...] = (q * (DH_A ** -0.5 * LOG2E)).astype(BF16)
    k = proj(N_QK, 2 * N_QK)
    kb_ref[...] = k.astype(BF16)
    v = proj(2 * N_QK, N_ATT)
    vb_ref[...] = v.astype(BF16)
    for h in range(H_A):
        k_ref[slab, :, :, h, :] = k[:, :, h * LANES:(h + 1) * LANES]
        v_ref[slab, :, :, h, :] = v[:, :, h * LANES:(h + 1) * LANES]
    for other in range(k_ref.shape[0]):
        if other != slab:
            k_ref[other] = jnp.zeros(k_ref.shape[1:], F32)
            v_ref[other] = jnp.zeros(v_ref.shape[1:], F32)
    prw_ref[...] = proj(N_ATT, N_ATT + N_RWKV)


def _mixin_call(x, mod, win, layer, depth, stacks):
    nseq, s, d = x.shape
    g_, ts = _seq_tiling(nseq, s)
    const = lambda i, j: (0, 0)
    blk = lambda w: pl.BlockSpec((g_, ts, w), lambda i, j: (i, j, 0))
    sds = lambda w, dt: jax.ShapeDtypeStruct((nseq, s, w), dt)
    if stacks is None:
        slab, slab_index = pl.BlockSpec((depth, g_, ts, H_A, LANES), lambda i, j: (0, i, j, 0, 0)), layer
    else:
        slab, slab_index = pl.BlockSpec((1, g_, ts, H_A, LANES), lambda i, j: (layer, i, j, 0, 0)), 0
    stack_sds = jax.ShapeDtypeStruct((depth, nseq, s, H_A, LANES), F32)
    in_specs = [blk(d),
                pl.BlockSpec((g_, 9, d), lambda i, j: (i, 0, 0)),
                pl.BlockSpec(win.shape, const, pipeline_mode=pl.Buffered(1))]
    args, aliases = [x, mod, win], {}
    if stacks is not None:
        in_specs += [pl.BlockSpec(memory_space=pl.ANY)] * 2
        args += list(stacks)
        aliases = {3: 1, 4: 2}
    return pl.pallas_call(
        functools.partial(_mixin_kernel, slab=slab_index),
        grid=(nseq // g_, s // ts),
        in_specs=in_specs,
        out_specs=[blk(N_QK), slab, slab, blk(N_QK), blk(N_QK), blk(N_RWKV)],
        out_shape=[sds(N_QK, BF16), stack_sds, stack_sds, sds(N_QK, BF16), sds(N_QK, BF16), sds(N_RWKV, F32)],
        input_output_aliases=aliases,
        compiler_params=_params("parallel", "parallel"),
        name="mixer_in",
    )(*args)


def _bias_tile(rb_ref, h, tq, tk, offset, valid_cols):
    r = lax.broadcasted_iota(jnp.int32, (tq, tk), 0)
    c = lax.broadcasted_iota(jnp.int32, (tq, tk), 1)
    rel = c - r + offset
    n = jnp.abs(rel)
    log_bucket = jnp.full((tq, tk), N_BUCKETS // 4, jnp.int32)
    for start in BUCKET_STARTS:
        log_bucket = log_bucket + jnp.where(n >= start, 1, 0)
    bucket = jnp.where(rel > 0, N_BUCKETS // 2, 0) + jnp.where(n < N_BUCKETS // 4, n, log_bucket)
    bias = jnp.zeros((tq, tk), F32)
    for b in range(N_BUCKETS):
        bias = jnp.where(bucket == b, rb_ref[b * H_A + h], bias)
    if offset == 0:
        visible = jnp.logical_and(_div_pow2(c, CHUNK) <= _div_pow2(r, CHUNK), c < valid_cols)
        bias = jnp.where(visible, bias, NEG_BIG)
    return bias


def _stack_maps(q):
    lane = lax.broadcasted_iota(jnp.int32, q.shape, 1)
    zero = jnp.zeros_like(q)
    return jnp.concatenate([jnp.where(lane < DH_A, q, zero), jnp.where(lane < DH_A, zero, q)], axis=0)


def _diff_head_out(o_all, tq, lam_ref, g_ref, lam_init):
    lq = lam_ref[...]
    lam = (jnp.exp(jnp.sum(lq[0:1] * lq[1:2], axis=1, keepdims=True))
           - jnp.exp(jnp.sum(lq[2:3] * lq[3:4], axis=1, keepdims=True)) + lam_init)
    o = o_all[:tq] - lam * o_all[tq:]
    o = o * lax.rsqrt(jnp.mean(o * o, axis=-1, keepdims=True) + LN_EPS)
    return o * g_ref[...] * (1.0 - lam_init)


def _attn_kernel(rb_ref, q_ref, k_ref, v_ref, lam_ref, g_ref, o_ref, m_scr, acc_scr, bias_scr,
                 *, tq, tk, lam_init):
    h = pl.program_id(1)
    i = pl.program_id(2)

    @pl.when(i == 0)
    def _():
        for slot, offset in ((0, -tk), (1, 0)):
            bias_scr[slot] = _bias_tile(rb_ref, h, tq, tk, offset, tk) * LOG2E

    q2 = _stack_maps(q_ref[0])
    m_scr[...] = jnp.full(m_scr.shape, NEG_BIG, F32)
    acc_scr[...] = jnp.zeros(acc_scr.shape, F32)
    ones = jnp.ones((tk, LANES), BF16)

    def tiles(js, bias_slots, shift):
        rows = 2 * tq // ROW_GROUPS
        starts = [pl.multiple_of(j * tk, tk) for j in js]

        def score(start, g):
            return lax.dot_general(q2[g * rows:(g + 1) * rows], k_ref[0, pl.ds(start, tk), :], _NT,
                                   preferred_element_type=F32)

        def update(start_v1, slot, g, s):
            sl = slice(g * rows, (g + 1) * rows)
            if slot is not None:
                r0 = (g * rows) % tq
                s = s + bias_scr[slot, r0:r0 + rows, :]
            m_prev = m_scr[sl]
            m_new = jnp.maximum(m_prev, jnp.max(s, axis=1, keepdims=True) + shift)
            p = jnp.exp2(s - jnp.tile(m_new - shift, (1, tk // LANES)))
            alpha = jnp.exp2(m_prev - m_new)
            acc_scr[sl] = (jnp.tile(alpha, (1, 2)) * acc_scr[sl]
                           + jnp.dot(p.astype(BF16), start_v1, preferred_element_type=F32))
            m_scr[sl] = m_new

        scores = [score(starts[0], g) for g in range(ROW_GROUPS)]
        for t, (start, slot) in enumerate(zip(starts, bias_slots)):
            v1 = jnp.concatenate([v_ref[0, pl.ds(start, tk), :], ones], axis=1)
            nxt = []
            for g in range(ROW_GROUPS):
                if t + 1 < len(starts):
                    nxt.append(score(starts[t + 1], g))
                update(v1, slot, g, scores[g])
            scores = nxt

    far_shift = rb_ref[FAR_BUCKET * H_A + h] * LOG2E
    n_far = jnp.maximum(i - 1, 0)

    def far_body(jj, carry):
        tiles([FAR_UNROLL * jj + t for t in range(FAR_UNROLL)], [None] * FAR_UNROLL, far_shift)
        return carry

    n_full = n_far // FAR_UNROLL
    lax.fori_loop(0, n_full, far_body, 0)
    for rem in range(1, FAR_UNROLL):
        @pl.when(n_far - n_full * FAR_UNROLL == rem)
        def _(rem=rem):
            tiles([n_full * FAR_UNROLL + t for t in range(rem)], [None] * rem, far_shift)

    @pl.when(i >= 1)
    def _():
        tiles([i - 1, i], [0, 1], 0.0)

    @pl.when(i == 0)
    def _():
        tiles([i], [1], 0.0)

    acc = acc_scr[...]
    o_ref[0] = _diff_head_out(acc[:, :DV_A] / acc[:, DV_A:], tq, lam_ref, g_ref, lam_init).astype(o_ref.dtype)


def _attn_call(q, k, v, rel_bias, lam_qk, subln_g, lam_init):
    b, s, _ = q.shape
    tq = tk = ATTN_TILE
    assert s % tk == 0 and tk % CHUNK == 0 and (2 * tq) % ROW_GROUPS == 0 and tq % (2 * tq // ROW_GROUPS) == 0
    kern = functools.partial(_attn_kernel, tq=tq, tk=tk, lam_init=lam_init)
    return pl.pallas_call(
        kern,
        grid=(b, H_A, s // tq),
        in_specs=[pl.BlockSpec(memory_space=pltpu.SMEM),
                  pl.BlockSpec((1, tq, LANES), lambda b_, h, i: (b_, i, h)),
                  pl.BlockSpec((1, s, LANES), lambda b_, h, i: (b_, 0, h)),
                  pl.BlockSpec((1, s, LANES), lambda b_, h, i: (b_, 0, h)),
                  pl.BlockSpec((4, DH_A), lambda b_, h, i: (0, 0)),
                  pl.BlockSpec((1, DV_A), lambda b_, h, i: (0, 0))],
        out_specs=pl.BlockSpec((1, tq, LANES), lambda b_, h, i: (b_, i, h)),
        out_shape=jax.ShapeDtypeStruct((b, s, H_A * DV_A), BF16),
        scratch_shapes=[pltpu.VMEM((2 * tq, LANES), F32), pltpu.VMEM((2 * tq, 2 * DV_A), F32),
                        pltpu.VMEM((2, tq, tk), F32)],
        compiler_params=_params("parallel", "parallel", "arbitrary"),
        name="diff_attn",
    )(rel_bias.reshape(-1), q, k, v, lam_qk, subln_g.reshape(1, DV_A))


def _attn_cached_kernel(rb_ref, q_ref, kc_ref, vc_ref, kn_ref, vn_ref, lam_ref, g_ref, o_ref,
                        *, sq, past, lam_init):
    near = min(past, LANES)
    for h in range(H_A):
        lanes = slice(h * LANES, (h + 1) * LANES)
        q2 = _stack_maps(q_ref[0, :, lanes])
        far = jnp.full((sq, past - near), rb_ref[FAR_BUCKET * H_A + h], F32)
        bias_p = jnp.concatenate([far, _bias_tile(rb_ref, h, sq, near, -near, near)], axis=1) * LOG2E
        bias_n = _bias_tile(rb_ref, h, sq, sq, 0, sq) * LOG2E
        kp = kc_ref[0, 0, pl.ds(h, past, stride=H_A), :].astype(BF16)
        vp = vc_ref[0, 0, pl.ds(h, past, stride=H_A), :].astype(BF16)
        sp = lax.dot_general(q2, kp, _NT, preferred_element_type=F32)
        sn = lax.dot_general(q2, kn_ref[0, :, lanes], _NT, preferred_element_type=F32)
        sp = (sp.reshape(2, sq, past) + bias_p).reshape(2 * sq, past)
        sn = (sn.reshape(2, sq, sq) + bias_n).reshape(2 * sq, sq)
        m = jnp.maximum(jnp.max(sp, axis=1, keepdims=True), jnp.max(sn, axis=1, keepdims=True))
        pp = jnp.exp2(sp - m)
        pn = jnp.exp2(sn - m)
        l = jnp.sum(pp, axis=1, keepdims=True) + jnp.sum(pn, axis=1, keepdims=True)
        acc = (jnp.dot(pp.astype(BF16), vp, preferred_element_type=F32)
               + jnp.dot(pn.astype(BF16), vn_ref[0, :, lanes], preferred_element_type=F32))
        o_ref[0, :, lanes] = _diff_head_out(acc / l, sq, lam_ref, g_ref, lam_init).astype(o_ref.dtype)


def _attn_cached_call(q, cache_k, cache_v, layer, k_new, v_new, rel_bias, lam_qk, subln_g, lam_init):
    b, sq, _ = q.shape
    depth, _, past = cache_k.shape[:3]
    assert past % CHUNK == 0 and past >= LANES and sq % 8 == 0
    kern = functools.partial(_attn_cached_kernel, sq=sq, past=past, lam_init=lam_init)
    new = pl.BlockSpec((1, sq, H_A * LANES), lambda b_: (b_, 0, 0))
    slab = pl.BlockSpec((1, 1, past * H_A, LANES), lambda b_: (layer, b_, 0, 0))
    rows = lambda c: c.reshape(depth, b, past * H_A, LANES)
    return pl.pallas_call(
        kern,
        grid=(b,),
        in_specs=[pl.BlockSpec(memory_space=pltpu.SMEM), new, slab, slab, new, new,
                  pl.BlockSpec((4, DH_A), lambda b_: (0, 0)),
                  pl.BlockSpec((1, DV_A), lambda b_: (0, 0))],
        out_specs=new,
        out_shape=jax.ShapeDtypeStruct((b, sq, H_A * DV_A), BF16),
        compiler_params=_params("parallel"),
        name="diff_attn_cached",
    )(rel_bias.reshape(-1), q, rows(cache_k), rows(cache_v), k_new, v_new, lam_qk, subln_g.reshape(1, DV_A))


def _split(x):
    hi = x.astype(BF16)
    return hi, (x - hi.astype(F32)).astype(BF16)


def _cat(pairs, axis):
    return tuple(jnp.concatenate(parts, axis=axis) for parts in zip(*pairs))


def _pdot(a, b, dims=_NN):
    (ah, al), (bh, bl) = a, b
    ca, cb = dims[0][0][0], dims[0][1][0]
    two = lax.dot_general(jnp.concatenate([ah, al], axis=ca), jnp.concatenate([bh, bh], axis=cb), dims,
                          preferred_element_type=F32)
    return two + lax.dot_general(ah, bl, dims, preferred_element_type=F32)


def _pdot_rhs16(a, b, dims=_NN):
    (ah, al), (bh, _) = a, b
    ca, cb = dims[0][0][0], dims[0][1][0]
    return lax.dot_general(jnp.concatenate([ah, al], axis=ca), jnp.concatenate([bh, bh], axis=cb), dims,
                           preferred_element_type=F32)


def _pdot_ones_rhs(a, ones):
    return jnp.dot(jnp.concatenate(a, axis=1), jnp.concatenate([ones, ones], axis=0),
                   preferred_element_type=F32)


def _pdot_ones_lhs(ones, b):
    return jnp.dot(jnp.concatenate([ones, ones], axis=1), jnp.concatenate(b, axis=0),
                   preferred_element_type=F32)


def _rwkv_kernel(prw_ref, shift_ref, s0_ref, mu_ref, wwa_hi_ref, wwa_lo_ref, g2_hi_ref, g2_lo_ref, vec_ref,
                 bd_ref, y_ref, sout_ref, carry_scr, state_scr, *, c_len):
    ci = pl.program_id(1)
    nc = pl.num_programs(1)
    n_pairs = H_B // 2
    n_seq = prw_ref.shape[0]

    @pl.when(ci == 0)
    def _():
        carry_scr[...] = shift_ref[...]
        state_scr[...] = s0_ref[...].reshape(state_scr.shape)

    row = lax.broadcasted_iota(jnp.int32, (c_len, N_RWKV), 0)
    xm = []
    for sq_i in range(n_seq):
        p = prw_ref[sq_i]
        prev = jnp.where(row == 0, carry_scr[sq_i], pltpu.roll(p, 1, axis=0))
        carry_scr[sq_i] = p[c_len - 1:c_len]
        xm.append(p + (prev - p) * mu_ref[...])
    xm = jnp.concatenate(xm, axis=0)

    w0, a0, k_k, k_a, r_k, lnx_g, lnx_b = (vec_ref[j:j + 1] for j in range(7))
    r = xm[:, :W_B]
    k = xm[:, W_B:2 * W_B]
    v = xm[:, 2 * W_B:3 * W_B]
    lora_in = xm[:, 3 * W_B:3 * W_B + W_LORA + A_LORA]
    lane = lax.broadcasted_iota(jnp.int32, lora_in.shape, 1)
    lora = _pdot(_split(jnp.where(lane < W_LORA, jnp.tanh(lora_in), lora_in)),
                 (wwa_hi_ref[...], wwa_lo_ref[...]))
    ww = -(w0 + lora[:, :W_B])
    w_log = -(jnp.maximum(ww, 0.0) + jnp.log(1.0 + jnp.exp(-jnp.abs(ww)))) - 0.5
    lw = -jnp.exp(w_log)
    a = _sigmoid(a0 + lora[:, W_B:])
    g = _pdot(_split(_sigmoid(xm[:, 3 * W_B + W_LORA + A_LORA:])), (g2_hi_ref[...], g2_lo_ref[...]))

    bd = bd_ref[...]

    def head_sum(x):
        return jnp.concatenate([_pdot_ones_rhs(_split(x[:, j * LANES:(j + 1) * LANES]), bd)
                                for j in range(W_B // LANES)], axis=1)

    kk = k * k_k
    kk = kk * lax.rsqrt(jnp.maximum(head_sum(kk * kk), 1e-24))
    km = k * (1.0 + (a - 1.0) * k_a)
    kb = kk * a
    bonus = head_sum(r * km * r_k) * v

    c2 = 2 * c_len
    tri_r = lax.broadcasted_iota(jnp.int32, (c_len, c_len), 0)
    tri_c = lax.broadcasted_iota(jnp.int32, (c_len, c_len), 1)
    tril = jnp.where(tri_r >= tri_c, 1.0, 0.0).astype(BF16)
    rr = lax.broadcasted_iota(jnp.int32, (c2, c2), 0)
    cc = lax.broadcasted_iota(jnp.int32, (c2, c2), 1)
    strict = rr > cc
    incl = rr >= cc
    eye = rr == cc
    same_blk = _div_pow2(rr, SOLVE_BLOCK) == _div_pow2(cc, SOLVE_BLOCK)
    head0 = lax.broadcasted_iota(jnp.int32, (c_len, LANES), 1) < DH_B

    def stack(x):
        z = jnp.zeros_like(x)
        return jnp.concatenate([jnp.where(head0, x, z), jnp.where(head0, z, x)], axis=0)

    def plus_eye(x):
        return jnp.where(eye, x + 1.0, x)

    def par(fn, *per_pair):
        return [fn(*args) for args in zip(*per_pair)]

    pairs = [(slice(i * c_len, (i + 1) * c_len), slice(j * LANES, (j + 1) * LANES))
             for i in range(n_seq) for j in range(n_pairs)]
    zero = jnp.zeros((c2, c2), F32)
    lw_p = [lw[sl] for sl in pairs]
    cum = par(lambda x: _pdot_ones_lhs(tril, _split(x)), lw_p)
    g_in = par(jnp.exp, cum)
    g_inv = par(lambda x: jnp.exp(-x), cum)
    a_hat = par(lambda sl, x, y: _split(stack(-kk[sl] * jnp.exp(x - y))), pairs, cum, lw_p)
    r_hat = par(lambda sl, x: _split(stack(r[sl] * x)), pairs, g_in)
    b_til = par(lambda sl, x: _split(stack(kb[sl] * x)), pairs, g_inv)
    k_til = par(lambda sl, x: _split(stack(km[sl] * x)), pairs, g_inv)
    v2 = par(lambda sl: _split(stack(v[sl])), pairs)
    lhs = par(lambda x, y: _cat([x, y], 0), a_hat, r_hat)
    if c2 % LANES == 0:
        both = par(lambda x, y, z: _pdot_rhs16(x, _cat([y, z], 0), _NT), lhs, b_til, k_til)
        ab_rb = [x[:, :c2] for x in both]
        ak_rk = [x[:, c2:] for x in both]
    else:
        ab_rb = par(lambda x, y: _pdot_rhs16(x, y, _NT), lhs, b_til)
        ak_rk = par(lambda x, y: _pdot_rhs16(x, y, _NT), lhs, k_til)
    a_ab = par(lambda x: jnp.where(strict, x[:c2], zero), ab_rb)
    a_rb = par(lambda x: _split(jnp.where(incl, x[c2:], zero)), ab_rb)
    a_ak = par(lambda x: _split(jnp.where(strict, x[:c2], zero)), ak_rk)
    a_rk = par(lambda x: _split(jnp.where(incl, x[c2:], zero)), ak_rk)

    d1 = par(lambda x: jnp.where(same_blk, x, zero), a_ab)
    dinv = par(plus_eye, d1)
    dpow = par(_split, d1)
    for _ in range(int(math.log2(SOLVE_BLOCK)) - 1):
        dpow = par(lambda x: _split(_pdot(x, x)), dpow)
        dinv = par(lambda x, y: x + _pdot(_split(x), y), dinv, dpow)
    npow = par(lambda x, y, z: _split(_pdot(_split(x), _split(y - z))), dinv, a_ab, d1)
    t_mat = dinv
    n_levels = max(1, int(math.ceil(math.log2(c_len // SOLVE_BLOCK))))
    for level in range(n_levels):
        t_mat = par(lambda x, y: y + _pdot(x, _split(y)), npow, t_mat)
        if level + 1 < n_levels:
            npow = par(lambda x: _split(_pdot(x, x)), npow)
    t_mat = par(_split, t_mat)

    s_pair = [state_scr[j] for j in range(len(pairs))]
    s_split = par(_split, s_pair)
    rhs = par(lambda x, y, z, w: _split(_pdot_rhs16(x, y, _NT) + _pdot_rhs16(z, w)), a_hat, s_split, a_ak, v2)
    u = par(lambda x, y: _split(_pdot(x, y)), t_mat, rhs)
    o2 = par(lambda x, y, z, w, p_, q_: _pdot_rhs16(x, y, _NT) + _pdot_rhs16(z, w) + _pdot_rhs16(p_, q_),
             r_hat, s_split, a_rk, v2, a_rb, u)
    outs = par(lambda x: x[:c_len] + x[c_len:], o2)
    s_add = par(lambda x, y, z, w: _pdot_rhs16(_cat([x, y], 0), _cat([z, w], 0), _TN), v2, u, k_til, b_til)
    for j in range(len(pairs)):
        state_scr[j] = (s_pair[j] + s_add[j]) * g_in[j][c_len - 1:c_len]

    o = jnp.concatenate([jnp.concatenate(outs[i * n_pairs:(i + 1) * n_pairs], axis=1)
                         for i in range(n_seq)], axis=0)
    inv_dh = 1.0 / DH_B
    mean = head_sum(o) * inv_dh
    oc = o - mean
    var = head_sum(oc * oc) * inv_dh
    on = oc * lax.rsqrt(var + GN_EPS) * lnx_g + lnx_b
    y_ref[...] = ((on + bonus) * g).reshape(y_ref.shape).astype(y_ref.dtype)

    @pl.when(ci == nc - 1)
    def _():
        sout_ref[...] = state_scr[...].reshape(sout_ref.shape)


def _pair_states(wkv):
    b = wkv.shape[0]
    w = wkv.reshape(b, H_B // 2, 2, DH_B, DH_B)
    z = jnp.zeros_like(w[:, :, 0])
    top = jnp.concatenate([w[:, :, 0], z], axis=-1)
    bot = jnp.concatenate([z, w[:, :, 1]], axis=-1)
    return jnp.concatenate([top, bot], axis=-2)


def _unpair_states(sp):
    b = sp.shape[0]
    return jnp.stack([sp[:, :, :DH_B, :DH_B], sp[:, :, DH_B:, DH_B:]], axis=2).reshape(b, H_B, DH_B, DH_B)


def _rwkv_call(prw, shift, wkv, mu, wwa, g2, vecs):
    nseq, s, _ = prw.shape
    c_len = min(CHUNK, s)
    n_seq = 2 if nseq % 2 == 0 else 1
    assert s % c_len == 0 and c_len % SOLVE_BLOCK == 0
    const2 = lambda b_, c: (0, 0)
    head_of = jnp.arange(LANES, dtype=jnp.int32) // DH_B
    bd = (head_of[:, None] == head_of[None, :]).astype(BF16)
    wwa_hi, wwa_lo = _split(wwa)
    g2_hi, g2_lo = _split(g2)
    y, s_out = pl.pallas_call(
        functools.partial(_rwkv_kernel, c_len=c_len),
        grid=(nseq // n_seq, s // c_len),
        in_specs=[pl.BlockSpec((n_seq, c_len, N_RWKV), lambda b_, c: (b_, c, 0)),
                  pl.BlockSpec((n_seq, 1, N_RWKV), lambda b_, c: (b_, 0, 0)),
                  pl.BlockSpec((n_seq, H_B // 2, LANES, LANES), lambda b_, c: (b_, 0, 0, 0)),
                  pl.BlockSpec((1, N_RWKV), const2),
                  pl.BlockSpec(wwa.shape, const2),
                  pl.BlockSpec(wwa.shape, const2),
                  pl.BlockSpec(g2.shape, const2),
                  pl.BlockSpec(g2.shape, const2),
                  pl.BlockSpec(vecs.shape, const2),
                  pl.BlockSpec(bd.shape, const2)],
        out_specs=[pl.BlockSpec((n_seq, c_len, W_B), lambda b_, c: (b_, c, 0)),
                   pl.BlockSpec((n_seq, H_B // 2, LANES, LANES), lambda b_, c: (b_, 0, 0, 0))],
        out_shape=[jax.ShapeDtypeStruct((nseq, s, W_B), BF16),
                   jax.ShapeDtypeStruct((nseq, H_B // 2, LANES, LANES), F32)],
        scratch_shapes=[pltpu.VMEM((n_seq, 1, N_RWKV), F32),
                        pltpu.VMEM((n_seq * (H_B // 2), LANES, LANES), F32)],
        compiler_params=_params("parallel", "arbitrary"),
        name="rwkv7_chunk",
    )(prw, shift.reshape(nseq, 1, N_RWKV), _pair_states(wkv), mu.reshape(1, N_RWKV), wwa_hi, wwa_lo, g2_hi, g2_lo,
      vecs, bd)
    return y, _unpair_states(s_out)


def _merge_kernel(x_ref, mod_ref, ya_ref, yb_ref, wg_ref, bg_ref, wa_ref, wb_ref, wo_ref, g_ref, b_ref,
                  o_ref, *, alpha):
    g_, ts, d = x_ref.shape
    m = g_ * ts
    x = x_ref[...]
    shift, scale, gate = _mod_rows(mod_ref, 1)
    u = (_norm(x) * (1.0 + scale) + shift).reshape(m, d).astype(BF16)
    ma = jnp.dot(ya_ref[...].reshape(m, -1), wa_ref[...], preferred_element_type=F32)
    mb = jnp.dot(yb_ref[...].reshape(m, -1), wb_ref[...], preferred_element_type=F32)
    ga = _sigmoid(jnp.dot(u, wg_ref[:, :d], preferred_element_type=F32) + bg_ref[:, :d])
    gb = _sigmoid(jnp.dot(u, wg_ref[:, d:], preferred_element_type=F32) + bg_ref[:, d:])
    merged = (ga * ma + gb * mb).astype(BF16)
    z = jnp.dot(merged, wo_ref[...], preferred_element_type=F32).reshape(g_, ts, d)
    o_ref[...] = _norm(alpha * x + gate * z) * g_ref[...] + b_ref[...]


def _merge_call(x, mod, ya, yb, wg, bg, wa, wb, wo, ln_g, ln_b, alpha):
    nseq, s, d = x.shape
    g_, ts = _seq_tiling(nseq, s)
    const = lambda i, j: (0, 0)
    blk = lambda w: pl.BlockSpec((g_, ts, w), lambda i, j: (i, j, 0))
    full = lambda a: pl.BlockSpec(a.shape, const, pipeline_mode=pl.Buffered(1))
    return pl.pallas_call(
        functools.partial(_merge_kernel, alpha=alpha),
        grid=(nseq // g_, s // ts),
        in_specs=[blk(d), pl.BlockSpec((g_, 9, d), lambda i, j: (i, 0, 0)), blk(ya.shape[-1]), blk(yb.shape[-1]),
                  full(wg), pl.BlockSpec((1, 2 * d), const), full(wa), full(wb), full(wo),
                  pl.BlockSpec((1, d), const), pl.BlockSpec((1, d), const)],
        out_specs=blk(d),
        out_shape=jax.ShapeDtypeStruct(x.shape, F32),
        compiler_params=_params("parallel", "parallel"),
        name="merge_out",
    )(x, mod, ya, yb, wg, bg.reshape(1, 2 * d), wa, wb, wo, ln_g.reshape(1, d), ln_b.reshape(1, d))


def _layer(x, mod, lw, rel_bias, caches, wkv, shift, lam_init, alpha, layer, depth, stacks):
    nseq, s, _ = x.shape
    x = _ffn_call(x, mod, lw["w_gu0"], lw["w_down0"], lw["ln_g"][0], lw["ln_b"][0], 0, alpha)
    q, k_stack, v_stack, kb, vb, prw = _mixin_call(x, mod, lw["w_in"], layer, depth, stacks)
    if caches is None:
        ya = _attn_call(q, kb, vb, rel_bias, lw["lam_qk"], lw["subln_g"], lam_init)
    else:
        ya = _attn_cached_call(q, caches[0], caches[1], layer, kb, vb, rel_bias, lw["lam_qk"], lw["subln_g"],
                               lam_init)
    yb, wkv_new = _rwkv_call(prw, shift, wkv, lw["rw_mu"], lw["rw_wwa"], lw["rw_g2"], lw["rw_vecs"])
    x = _merge_call(x, mod, ya, yb, lw["w_gate"], lw["b_gate"], lw["w_br_a"], lw["w_br_b"], lw["w_o"],
                    lw["ln_g"][1], lw["ln_b"][1], alpha)
    x = _ffn_call(x, mod, lw["w_gu1"], lw["w_down1"], lw["ln_g"][2], lw["ln_b"][2], 2, alpha)
    return x, (k_stack, v_stack), wkv_new, prw[:, -1]


def kernel(x_prompt, x_sample, c_prompt, c_sample, cache_k, cache_v, state_wkv, state_shift, rel_bias, w_ada, b_ada, ln_g, ln_b, w_gu, w_down, w_in, lam_qk, subln_g, rw_mu, rw_w0, rw_w2, rw_a0, rw_a2, rw_g2, rw_k_k, rw_k_a, rw_r_k, rw_lnx_g, rw_lnx_b, w_br_a, w_br_b, w_gate, b_gate, w_o):
    depth = w_ada.shape[0]
    b_p = x_prompt.shape[0]
    alpha = (2 * depth) ** 0.25
    mod = _ada_call(jnp.concatenate([c_prompt, c_sample], axis=0), w_ada, b_ada)
    mod = mod.reshape(depth, -1, 9, D_MODEL)

    w_gu_b, w_down_b, w_in_b = w_gu.astype(BF16), w_down.astype(BF16), w_in.astype(BF16)
    w_gate_b, w_br_a_b, w_br_b_b, w_o_b = (w.astype(BF16) for w in (w_gate, w_br_a, w_br_b, w_o))
    zeros_l = jnp.zeros((depth, W_LORA, W_B), F32)
    rw_wwa = jnp.concatenate([jnp.concatenate([rw_w2, zeros_l], axis=2),
                              jnp.concatenate([zeros_l, rw_a2], axis=2)], axis=1)
    rw_vecs = jnp.stack([rw_w0, rw_a0, rw_k_k, rw_k_a, rw_r_k.reshape(depth, W_B), rw_lnx_g, rw_lnx_b,
                         jnp.zeros_like(rw_w0)], axis=1)

    zero_wkv = jnp.zeros((b_p, H_B, DH_B, DH_B), F32)
    zero_shift = jnp.zeros((b_p, N_RWKV), F32)
    hp, hd = x_prompt, x_sample
    kv_p = kv_d = None
    wkv_p, wkv_d, shift_p, shift_d = [], [], [], []
    for l in range(depth):
        lw = {"w_gu0": w_gu_b[l, 0], "w_gu1": w_gu_b[l, 1], "w_down0": w_down_b[l, 0], "w_down1": w_down_b[l, 1],
              "w_in": w_in_b[l], "ln_g": ln_g[l], "ln_b": ln_b[l], "lam_qk": lam_qk[l], "subln_g": subln_g[l],
              "rw_mu": rw_mu[l], "rw_wwa": rw_wwa[l], "rw_g2": rw_g2[l], "rw_vecs": rw_vecs[l],
              "w_gate": w_gate_b[l], "b_gate": b_gate[l], "w_br_a": w_br_a_b[l], "w_br_b": w_br_b_b[l],
              "w_o": w_o_b[l]}
        lam_init = 0.8 - 0.6 * math.exp(-0.3 * l)
        hp, kv_p, wkv, shift = _layer(hp, mod[l, :b_p], lw, rel_bias, None, zero_wkv, zero_shift,
                                      lam_init, alpha, l, depth, kv_p)
        wkv_p.append(wkv)
        shift_p.append(shift)
        hd, kv_d, wkv, shift = _layer(hd, mod[l, b_p:], lw, rel_bias, (cache_k, cache_v), state_wkv[l],
                                      state_shift[l], lam_init, alpha, l, depth, kv_d)
        wkv_d.append(wkv)
        shift_d.append(shift)
    return (hp, hd, kv_p[0], kv_p[1], jnp.stack(wkv_p), jnp.stack(shift_p),
            kv_d[0], kv_d[1], jnp.stack(wkv_d), jnp.stack(shift_d))
```
